```python
import math
import jax, jax.numpy as jnp
from jax import lax
import numpy as np

D_MODEL = 1024
BATCH = 16
SEQ = 4096
DEPTH = 4

GRID_W = 64
N_MIXERS = 2
N_ATTN_LAYERS = (DEPTH + 1) // 2
N_HYENA_LAYERS = DEPTH // 2
N_Q_HEADS = 16
N_KV_HEADS = 4
HEAD_DIM = D_MODEL // N_Q_HEADS
Q_PER_KV = N_Q_HEADS // N_KV_HEADS
D_ATTN = N_Q_HEADS * HEAD_DIM
D_KV = N_KV_HEADS * HEAD_DIM
D_QKV = D_ATTN + 2 * D_KV
ROPE_THETA = 10000.0
ROPE_AXIS_DIM = HEAD_DIM // 2
Q_BLOCK = 128
D_HYENA = D_MODEL
SHORT_CONV = 3
POS_EMB_DIM = 33
FILTER_HIDDEN = 64
DECAY_TARGET = 1e-2
FAST_DECAY_PCT = 0.3
SLOW_DECAY_PCT = 1.5
D_FF = 2816
FFN_RES_SCALE = 0.5
NORM_EPS = 1e-6

kernel_name = 'hybrid_gqa_hyena_macaron_encoder'


def rms_norm(x, g):
    x32 = x.astype(jnp.float32)
    y = x32 * lax.rsqrt(jnp.mean(x32 * x32, axis=-1, keepdims=True) + NORM_EPS)
    return (y * g.astype(jnp.float32)).astype(x.dtype)


def swiglu(h, w_in, w_out):
    gate, up = jnp.split(h @ w_in, 2, axis=-1)
    return (jax.nn.silu(gate) * up) @ w_out


def axial_rope_tables(L):
    rows = L // GRID_W
    row = jnp.repeat(jnp.arange(rows, dtype=jnp.float32), GRID_W)
    col = jnp.tile(jnp.arange(GRID_W, dtype=jnp.float32), rows)
    inv = ROPE_THETA ** (-jnp.arange(0, ROPE_AXIS_DIM, 2, dtype=jnp.float32) / ROPE_AXIS_DIM)
    ang = jnp.concatenate([row[:, None] * inv, col[:, None] * inv], axis=-1)
    return jnp.cos(ang), jnp.sin(ang)


def apply_rope(x, cos, sin):
    shape = (1, x.shape[1]) + (1,) * (x.ndim - 3) + (HEAD_DIM // 2,)
    c = cos.reshape(shape)
    s = sin.reshape(shape)
    xp = x.reshape(x.shape[:-1] + (HEAD_DIM // 2, 2))
    x1, x2 = xp[..., 0], xp[..., 1]
    return jnp.stack([x1 * c - x2 * s, x1 * s + x2 * c], axis=-1).reshape(x.shape)


def attention_mixer(h, w_in, q_gain, k_gain, w_out, cos, sin):
    B, L, _ = h.shape
    qkv = h @ w_in
    q = qkv[..., :D_ATTN].reshape(B, L, N_KV_HEADS, Q_PER_KV, HEAD_DIM)
    k = qkv[..., D_ATTN:D_ATTN + D_KV].reshape(B, L, N_KV_HEADS, HEAD_DIM)
    v = qkv[..., D_ATTN + D_KV:].reshape(B, L, N_KV_HEADS, HEAD_DIM)
    q = (apply_rope(rms_norm(q, q_gain).astype(jnp.float32), cos, sin) * (HEAD_DIM ** -0.5)).astype(h.dtype)
    k = apply_rope(rms_norm(k, k_gain).astype(jnp.float32), cos, sin).astype(h.dtype)
    nb = L // Q_BLOCK
    q_blocks = q.reshape(B, nb, Q_BLOCK, N_KV_HEADS, Q_PER_KV, HEAD_DIM).transpose(1, 0, 2, 3, 4, 5)

    def one_block(qb):
        s = jnp.einsum('bqkgd,bskd->bkgqs', qb, k, preferred_element_type=jnp.float32)
        p = jax.nn.softmax(s, axis=-1).astype(v.dtype)
        return jnp.einsum('bkgqs,bskd->bqkgd', p, v)

    o = lax.map(one_block, q_blocks)
    o = o.transpose(1, 0, 2, 3, 4, 5).reshape(B, L, D_ATTN)
    return o @ w_out


def hyena_positions(L):
    t = jnp.linspace(0.0, 1.0, L, dtype=jnp.float32)[:, None]
    bands = (POS_EMB_DIM - 1) // 2
    f = jnp.linspace(1e-4, bands - 1, bands, dtype=jnp.float32)
    w = 2.0 * math.pi * jnp.arange(L, dtype=jnp.float32)[:, None] / L
    z = jnp.concatenate([t, jnp.cos(f * w), -jnp.sin(f * w)], axis=-1)
    return z, t


def hyena_filter(z, t, w1, b1, w2, b2, w3, b3, freq, w_out, decay):
    L = z.shape[0]
    fr = freq.astype(jnp.float32)
    hid = jnp.sin(fr * (z @ w1.astype(jnp.float32) + b1.astype(jnp.float32)))
    hid = jnp.sin(fr * (hid @ w2.astype(jnp.float32) + b2.astype(jnp.float32)))
    hid = jnp.sin(fr * (hid @ w3.astype(jnp.float32) + b3.astype(jnp.float32)))
    filt = (hid @ w_out.astype(jnp.float32)).reshape(L, 2, D_HYENA)
    filt = filt * jnp.exp(-t[:, :, None] * jnp.abs(decay.astype(jnp.float32))[None])
    h_fwd, h_bwd = filt[:, 0], filt[:, 1]
    k2 = jnp.concatenate([h_fwd, jnp.zeros((1, D_HYENA), jnp.float32), h_bwd[:0:-1]], axis=0)
    return k2 / jnp.sum(jnp.abs(k2), axis=0, keepdims=True)


def hyena_mixer(h, w_in, b_in, conv_w, conv_b, k2, d_bias, w_out, b_out):
    B, L, _ = h.shape
    u = h @ w_in + b_in
    pad = SHORT_CONV // 2
    up = jnp.pad(u, ((0, 0), (pad, pad), (0, 0)))
    u = sum(conv_w[j] * up[:, j:j + L] for j in range(SHORT_CONV)) + conv_b
    x0, x1, v = jnp.split(u, 3, axis=-1)
    v = (v * x1).astype(jnp.float32)
    n = 2 * L
    y = jnp.fft.irfft(jnp.fft.rfft(v, n=n, axis=1) * jnp.fft.rfft(k2, n=n, axis=0)[None], n=n, axis=1)[:, :L]
    y = y + d_bias.astype(jnp.float32) * v
    return (y.astype(h.dtype) * x0) @ w_out + b_out


def setup_inputs(seed: int = 0) -> dict:
    key = jax.random.key(seed)
    ks = jax.random.split(key, 32)

    def nrm(k, shape, scale):
        return jax.random.normal(k, shape, jnp.float32) * scale

    min_decay = math.log(DECAY_TARGET) / SLOW_DECAY_PCT
    max_decay = math.log(DECAY_TARGET) / FAST_DECAY_PCT
    base_decay = jnp.linspace(min_decay, max_decay, D_HYENA, dtype=jnp.float32)
    nh = N_HYENA_LAYERS
    na = N_ATTN_LAYERS
    return {
        'x': nrm(ks[0], (BATCH, SEQ, D_MODEL), 1.0),
        'ffn_norm_g': 1.0 + nrm(ks[1], (DEPTH, 2, D_MODEL), 0.02),
        'mix_norm_g': 1.0 + nrm(ks[2], (DEPTH, D_MODEL), 0.02),
        'ffn_w_in': nrm(ks[3], (DEPTH, 2, D_MODEL, 2 * D_FF), D_MODEL ** -0.5),
        'ffn_w_out': nrm(ks[4], (DEPTH, 2, D_FF, D_MODEL), D_FF ** -0.5),
        'attn_w_in': nrm(ks[5], (na, D_MODEL, D_QKV), D_MODEL ** -0.5),
        'attn_q_gain': 1.0 + nrm(ks[6], (na, HEAD_DIM), 0.02),
        'attn_k_gain': 1.0 + nrm(ks[7], (na, HEAD_DIM), 0.02),
        'attn_w_out': nrm(ks[8], (na, D_ATTN, D_MODEL), D_ATTN ** -0.5),
        'hy_w_in': nrm(ks[9], (nh, D_MODEL, 3 * D_HYENA), D_MODEL ** -0.5),
        'hy_b_in': nrm(ks[10], (nh, 3 * D_HYENA), 0.02),
        'hy_conv_w': nrm(ks[11], (nh, SHORT_CONV, 3 * D_HYENA), SHORT_CONV ** -0.5),
        'hy_conv_b': nrm(ks[12], (nh, 3 * D_HYENA), 0.02),
        'hy_f_w1': nrm(ks[13], (nh, POS_EMB_DIM, FILTER_HIDDEN), POS_EMB_DIM ** -0.5),
        'hy_f_b1': nrm(ks[14], (nh, FILTER_HIDDEN), 0.1),
        'hy_f_w2': nrm(ks[15], (nh, FILTER_HIDDEN, FILTER_HIDDEN), FILTER_HIDDEN ** -0.5),
        'hy_f_b2': nrm(ks[16], (nh, FILTER_HIDDEN), 0.1),
        'hy_f_w3': nrm(ks[17], (nh, FILTER_HIDDEN, FILTER_HIDDEN), FILTER_HIDDEN ** -0.5),
        'hy_f_b3': nrm(ks[18], (nh, FILTER_HIDDEN), 0.1),
        'hy_f_freq': 1.0 + nrm(ks[19], (nh, FILTER_HIDDEN), 0.01),
        'hy_f_w_out': nrm(ks[20], (nh, FILTER_HIDDEN, 2 * D_HYENA), FILTER_HIDDEN ** -0.5),
        'hy_decay': base_decay + nrm(ks[21], (nh, 2, D_HYENA), 0.1),
        'hy_d_bias': nrm(ks[22], (nh, D_HYENA), 1.0),
        'hy_w_out': nrm(ks[23], (nh, D_HYENA, D_MODEL), D_HYENA ** -0.5),
        'hy_b_out': nrm(ks[24], (nh, D_MODEL), 0.02),
        'final_norm_g': 1.0 + nrm(ks[25], (D_MODEL,), 0.02),
    }


def reference(x, ffn_norm_g, mix_norm_g, ffn_w_in, ffn_w_out, attn_w_in, attn_q_gain, attn_k_gain,
              attn_w_out, hy_w_in, hy_b_in, hy_conv_w, hy_conv_b, hy_f_w1, hy_f_b1, hy_f_w2, hy_f_b2,
              hy_f_w3, hy_f_b3, hy_f_freq, hy_f_w_out, hy_decay, hy_d_bias, hy_w_out, hy_b_out,
              final_norm_g):
    L = x.shape[1]
    cos, sin = axial_rope_tables(L)
    z_pos, t_pos = hyena_positions(L)
    for i in range(DEPTH):
        x = x + FFN_RES_SCALE * swiglu(rms_norm(x, ffn_norm_g[i, 0]), ffn_w_in[i, 0], ffn_w_out[i, 0])
        h = rms_norm(x, mix_norm_g[i])
        j = i // N_MIXERS
        if i % N_MIXERS == 0:
            y = attention_mixer(h, attn_w_in[j], attn_q_gain[j], attn_k_gain[j], attn_w_out[j], cos, sin)
        else:
            k2 = hyena_filter(z_pos, t_pos, hy_f_w1[j], hy_f_b1[j], hy_f_w2[j], hy_f_b2[j], hy_f_w3[j],
                              hy_f_b3[j], hy_f_freq[j], hy_f_w_out[j], hy_decay[j])
            y = hyena_mixer(h, hy_w_in[j], hy_b_in[j], hy_conv_w[j], hy_conv_b[j], k2, hy_d_bias[j],
                            hy_w_out[j], hy_b_out[j])
        x = x + y
        x = x + FFN_RES_SCALE * swiglu(rms_norm(x, ffn_norm_g[i, 1]), ffn_w_in[i, 1], ffn_w_out[i, 1])
    return rms_norm(x, final_norm_g)
```

```python
import functools
import math

import jax
import jax.numpy as jnp
from jax import lax
from jax.experimental import pallas as pl
from jax.experimental.pallas import tpu as pltpu

F32 = jnp.float32
BF16 = jnp.bfloat16

NORM_EPS = 1e-6
FFN_RES_SCALE = 0.5
ROPE_THETA = 10000.0
GRID_W = 64
N_Q_HEADS = 16
N_KV_HEADS = 4
SHORT_CONV = 3
DECAY_POS_COL = 33
MASK_POS_COL = 34
POS_PAD = 40
CONV_BLOCK = 1024

LANES = 128
SUBLANES = 8
VMEM_LIMIT = 56 * 1024 * 1024

HIGHEST = lax.Precision.HIGHEST


def _cparams(sem):
    return pltpu.CompilerParams(dimension_semantics=sem, vmem_limit_bytes=VMEM_LIMIT)


def _resident(shape, index_map):
    return pl.BlockSpec(shape, index_map, pipeline_mode=pl.Buffered(1))


def _rms(x, g):
    ms = jnp.mean(x * x, axis=-1, keepdims=True)
    return x * lax.rsqrt(ms + NORM_EPS) * g


def _dot(a, b):
    return jnp.dot(a, b, preferred_element_type=F32)


def _ffn_kernel(x_ref, g_ref, win_ref, wout_ref, gf_ref, o_ref, acc_ref, *, d_ff, fc, final):
    x = x_ref[...]
    h = _rms(x, g_ref[...]).astype(BF16)
    for c in range(d_ff // fc):
        gate = _dot(h, win_ref[:, c * fc:(c + 1) * fc])
        up = _dot(h, win_ref[:, d_ff + c * fc:d_ff + (c + 1) * fc])
        a = (gate * jax.nn.sigmoid(gate) * up).astype(BF16)
        part = _dot(a, wout_ref[c * fc:(c + 1) * fc, :])
        if c == 0:
            acc_ref[...] = part
        else:
            acc_ref[...] += part
    y = x + FFN_RES_SCALE * acc_ref[...]
    if final:
        y = _rms(y, gf_ref[...])
    o_ref[...] = y


def _ffn(x2, g, w_in, w_out, gf, final, tm=512, fc=256):
    t, d = x2.shape
    d_ff = w_out.shape[0]
    kern = functools.partial(_ffn_kernel, d_ff=d_ff, fc=fc, final=final)
    return pl.pallas_call(
        kern,
        grid=(t // tm,),
        in_specs=[
            pl.BlockSpec((tm, d), lambda i: (i, 0)),
            _resident((1, d), lambda i: (0, 0)),
            _resident((d, 2 * d_ff), lambda i: (0, 0)),
            _resident((d_ff, d), lambda i: (0, 0)),
            _resident((1, d), lambda i: (0, 0)),
        ],
        out_specs=pl.BlockSpec((tm, d), lambda i: (i, 0)),
        out_shape=jax.ShapeDtypeStruct((t, d), F32),
        scratch_shapes=[pltpu.VMEM((tm, d), F32)],
        compiler_params=_cparams(("arbitrary",)),
        name="ffn",
    )(x2, g, w_in, w_out, gf)


def _head_rms_scale(q, gsum, gbc, head_dim):
    ss = _dot((q * q).astype(BF16), gsum)
    r = lax.rsqrt(ss * (1.0 / head_dim) + NORM_EPS)
    r_hi = r.astype(BF16)
    r_lo = (r - r_hi.astype(F32)).astype(BF16)
    return _dot(jnp.concatenate([r_hi, r_lo], axis=-1), gbc)


def _rope(xn, c_tab, so_tab, se_tab):
    w = xn.shape[-1]
    reps = w // LANES
    c = jnp.concatenate([c_tab] * reps, axis=-1)
    so = jnp.concatenate([so_tab] * reps, axis=-1)
    se = jnp.concatenate([se_tab] * reps, axis=-1)
    return xn * c + pltpu.roll(xn, 1, axis=1) * so + pltpu.roll(xn, w - 1, axis=1) * se


def _qkv_kernel(x_ref, g_ref, w_ref, qg_ref, kg_ref, gsq_ref, gbq_ref, gsk_ref, gbk_ref,
                cq_ref, soq_ref, seq_ref, ck_ref, sok_ref, sek_ref,
                qt_ref, k_ref, vt_ref, *, d_attn, d_kv, head_dim):
    h = _rms(x_ref[0], g_ref[...]).astype(BF16)
    qkv = _dot(h, w_ref[...])
    q = qkv[:, :d_attn]
    k = qkv[:, d_attn:d_attn + d_kv]
    v = qkv[:, d_attn + d_kv:]
    qn = q * _head_rms_scale(q, gsq_ref[...], gbq_ref[...], head_dim) * qg_ref[...]
    kn = k * _head_rms_scale(k, gsk_ref[...], gbk_ref[...], head_dim) * kg_ref[...]
    qr = _rope(qn, cq_ref[...], soq_ref[...], seq_ref[...])
    kr = _rope(kn, ck_ref[...], sok_ref[...], sek_ref[...])
    qt_ref[0] = qr.T.astype(BF16)
    for gidx in range(d_kv // head_dim):
        k_ref[0, gidx] = kr[:, gidx * head_dim:(gidx + 1) * head_dim].astype(BF16)
    vt_ref[0] = v.T.astype(BF16)


def _attn_kernel(x_ref, qt_ref, k_ref, vt_ref, wot_ref, o_ref, ot_ref, *, n_heads, q_per_kv,
                 head_dim):
    def one_head(hd, carry):
        grp = hd // q_per_kv
        s = _dot(k_ref[0, grp], qt_ref[0, hd])
        m = jnp.max(s, axis=0, keepdims=True)
        p = jnp.exp2(s - m)
        l = jnp.sum(p, axis=0, keepdims=True)
        o = _dot(vt_ref[0, grp], p.astype(BF16))
        row = pl.multiple_of(hd * head_dim, head_dim)
        ot_ref[pl.ds(row, head_dim), :] = (o / l).astype(BF16)
        return carry

    lax.fori_loop(0, n_heads, one_head, 0)
    yt = _dot(wot_ref[...], ot_ref[...])
    o_ref[0] = x_ref[0] + yt.T


def _attention_layer(x, g, w_in, q_gain, k_gain, w_out, consts, tm=512, tq=256):
    b, l, d = x.shape
    head_dim = q_gain.shape[0]
    d_attn = w_out.shape[0]
    d_kv = (w_in.shape[1] - d_attn) // 2
    n_heads = d_attn // head_dim
    n_kv = d_kv // head_dim
    tm = min(tm, l)
    tq = min(tq, l)
    tab = pl.BlockSpec((tm, LANES), lambda bi, i: (i, 0))
    const = lambda shape: _resident(shape, lambda bi, i: (0,) * len(shape))
    qkv_kern = functools.partial(_qkv_kernel, d_attn=d_attn, d_kv=d_kv, head_dim=head_dim)
    qt, k, vt = pl.pallas_call(
        qkv_kern,
        grid=(b, l // tm),
        in_specs=[
            pl.BlockSpec((1, tm, d), lambda bi, i: (bi, i, 0)),
            const((1, d)),
            const((d, d_attn + 2 * d_kv)),
            const((1, d_attn)),
            const((1, d_kv)),
            const((d_attn, LANES)),
            const((2 * LANES, d_attn)),
            const((d_kv, LANES)),
            const((2 * LANES, d_kv)),
            tab, tab, tab, tab, tab, tab,
        ],
        out_specs=[
            pl.BlockSpec((1, d_attn, tm), lambda bi, i: (bi, 0, i)),
            pl.BlockSpec((1, n_kv, tm, head_dim), lambda bi, i: (bi, 0, i, 0)),
            pl.BlockSpec((1, d_kv, tm), lambda bi, i: (bi, 0, i)),
        ],
        out_shape=[
            jax.ShapeDtypeStruct((b, d_attn, l), BF16),
            jax.ShapeDtypeStruct((b, n_kv, l, head_dim), BF16),
            jax.ShapeDtypeStruct((b, d_kv, l), BF16),
        ],
        compiler_params=_cparams(("arbitrary", "arbitrary")),
        name="attn_qkv",
    )(x, g, w_in.astype(BF16),
      jnp.tile(q_gain, n_heads)[None, :], jnp.tile(k_gain, n_kv)[None, :],
      consts["gsum_q"], consts["gbc_q"], consts["gsum_k"], consts["gbc_k"],
      consts["cq"], consts["soq"], consts["seq"], consts["ck"], consts["sok"], consts["sek"])

    qt = qt.reshape(b, n_heads, head_dim, l)
    vt = vt.reshape(b, n_kv, head_dim, l)
    attn_kern = functools.partial(_attn_kernel, n_heads=n_heads, q_per_kv=n_heads // n_kv,
                                  head_dim=head_dim)
    return pl.pallas_call(
        attn_kern,
        grid=(b, l // tq),
        in_specs=[
            pl.BlockSpec((1, tq, d), lambda bi, i: (bi, i, 0)),
            pl.BlockSpec((1, n_heads, head_dim, tq), lambda bi, i: (bi, 0, 0, i)),
            pl.BlockSpec((1, n_kv, l, head_dim), lambda bi, i: (bi, 0, 0, 0)),
            pl.BlockSpec((1, n_kv, head_dim, l), lambda bi, i: (bi, 0, 0, 0)),
            _resident((d, d_attn), lambda bi, i: (0, 0)),
        ],
        out_specs=pl.BlockSpec((1, tq, d), lambda bi, i: (bi, i, 0)),
        out_shape=jax.ShapeDtypeStruct((b, l, d), F32),
        scratch_shapes=[pltpu.VMEM((d_attn, tq), BF16)],
        compiler_params=_cparams(("arbitrary", "arbitrary")),
        name="attn_core",
    )(x, qt, k, vt, w_out.T.astype(BF16))


def _attention_consts(l, head_dim, n_heads, n_kv):
    rows = l // GRID_W
    axis_dim = head_dim // 2
    row = jnp.repeat(jnp.arange(rows, dtype=F32), GRID_W)
    col = jnp.tile(jnp.arange(GRID_W, dtype=F32), rows)
    inv = ROPE_THETA ** (-jnp.arange(0, axis_dim, 2, dtype=F32) / axis_dim)
    ang = jnp.concatenate([row[:, None] * inv, col[:, None] * inv], axis=-1)
    cos = jnp.repeat(jnp.cos(ang), 2, axis=-1)
    sin = jnp.repeat(jnp.sin(ang), 2, axis=-1)
    odd = (jnp.arange(head_dim) % 2 == 1)[None, :]
    s_odd = jnp.where(odd, sin, 0.0)
    s_even = jnp.where(odd, 0.0, -sin)
    reps = LANES // head_dim
    wide = lambda a: jnp.tile(a, (1, reps))
    q_scale = (head_dim ** -0.5) * math.log2(math.e)
    out = {
        "cq": wide(cos) * q_scale, "soq": wide(s_odd) * q_scale, "seq": wide(s_even) * q_scale,
        "ck": wide(cos), "sok": wide(s_odd), "sek": wide(s_even),
    }
    for name, nh in (("q", n_heads), ("k", n_kv)):
        lane_head = jnp.arange(nh * head_dim) // head_dim
        gsum = (lane_head[:, None] == jnp.arange(LANES)[None, :]).astype(BF16)
        out["gsum_" + name] = gsum
        out["gbc_" + name] = jnp.concatenate([gsum.T, gsum.T], axis=0)
    return out


def _filter_taps_kernel(zk_ref, w1_ref, b1_ref, w2_ref, b2_ref, w3_ref, b3_ref, fr_ref, wo_ref,
                        dec_ref, kap_ref, sum_ref):
    zk = zk_ref[...]
    fr = fr_ref[...]
    hid = jnp.sin(fr * (jnp.dot(zk, w1_ref[...], precision=HIGHEST, preferred_element_type=F32)
                        + b1_ref[...]))
    hid = jnp.sin(fr * (jnp.dot(hid, w2_ref[...], precision=HIGHEST, preferred_element_type=F32)
                        + b2_ref[...]))
    hid = jnp.sin(fr * (jnp.dot(hid, w3_ref[...], precision=HIGHEST, preferred_element_type=F32)
                        + b3_ref[...]))
    filt = jnp.dot(hid, wo_ref[...], precision=HIGHEST, preferred_element_type=F32)
    t = zk[:, DECAY_POS_COL:DECAY_POS_COL + 1]
    mask = zk[:, MASK_POS_COL:MASK_POS_COL + 1]
    kap = filt * jnp.exp(-t * jnp.abs(dec_ref[0])) * mask
    kap_ref[...] = kap

    @pl.when(pl.program_id(0) == 0)
    def _():
        sum_ref[...] = jnp.zeros_like(sum_ref)

    sum_ref[...] += jnp.sum(jnp.abs(kap), axis=0, keepdims=True)


def _filter_spec_kernel(f_ref, klo_ref, khi_ref, sum_ref, dbias_ref, c_ref, *, cb, nb):
    dp = pl.program_id(0)
    inv_sum = 1.0 / sum_ref[...]
    dbias = dbias_ref[...]

    def spectrum(k_ref, is_zero_lag_block):
        kap = k_ref[...]
        hi = kap.astype(BF16)
        lo = (kap - hi.astype(F32)).astype(BF16)
        a = (_dot(f_ref[...], hi) + _dot(f_ref[...], lo)) * inv_sum
        return a[:cb] + jnp.where(is_zero_lag_block, dbias, 0.0), a[cb:]

    ar1, ai1 = spectrum(khi_ref, dp + 1 == nb)
    ar0, ai0 = spectrum(klo_ref, dp == nb)
    kidx = lax.broadcasted_iota(jnp.int32, ar1.shape, 0)
    sgn = (1 - 2 * (kidx % 2)).astype(F32)
    c_ref[0, :cb, :] = ar1 - sgn * ai0
    c_ref[0, cb:, :] = ai1 + sgn * ar0


def _hyena_filter_spectra(consts, w1, b1, w2, b2, w3, b3, freq, w_out, decay, d_bias, l, cb,
                          tr=512, dtile=512):
    d = d_bias.shape[0]
    nb = l // cb
    tr = min(tr, l)
    hidden = w2.shape[0]
    w1p = jnp.zeros((POS_PAD, hidden), F32).at[:w1.shape[0]].set(w1)
    half_steps = l // tr
    vec = lambda a: a[None, :]
    const = lambda shape: _resident(shape, lambda i: (0,) * len(shape))
    kap, ksum = pl.pallas_call(
        _filter_taps_kernel,
        grid=(2 * l // tr,),
        in_specs=[
            pl.BlockSpec((tr, POS_PAD), lambda i: (i, 0)),
            const((POS_PAD, hidden)), const((1, hidden)),
            const((hidden, hidden)), const((1, hidden)),
            const((hidden, hidden)), const((1, hidden)),
            const((1, hidden)),
            pl.BlockSpec((hidden, d), lambda i: (0, jnp.where(i < half_steps, 1, 0))),
            pl.BlockSpec((1, 1, d), lambda i: (jnp.where(i < half_steps, 1, 0), 0, 0)),
        ],
        out_specs=[
            pl.BlockSpec((tr, d), lambda i: (i, 0)),
            pl.BlockSpec((1, d), lambda i: (0, 0)),
        ],
        out_shape=[
            jax.ShapeDtypeStruct((2 * l, d), F32),
            jax.ShapeDtypeStruct((1, d), F32),
        ],
        compiler_params=_cparams(("arbitrary",)),
        name="hyena_filter_taps",
    )(consts["zk"], w1p, vec(b1), w2, vec(b2), w3, vec(b3), vec(freq), w_out,
      decay[:, None, :])

    dtile = min(dtile, d)
    spec_kern = functools.partial(_filter_spec_kernel, cb=cb, nb=nb)
    return pl.pallas_call(
        spec_kern,
        grid=(2 * nb - 1, d // dtile),
        in_specs=[
            _resident((2 * cb, cb), lambda dp, j: (0, 0)),
            pl.BlockSpec((cb, dtile), lambda dp, j: (dp, j)),
            pl.BlockSpec((cb, dtile), lambda dp, j: (dp + 1, j)),
            pl.BlockSpec((1, dtile), lambda dp, j: (0, j)),
            pl.BlockSpec((1, dtile), lambda dp, j: (0, j)),
        ],
        out_specs=pl.BlockSpec((1, 2 * cb, dtile), lambda dp, j: (dp, 0, j)),
        out_shape=jax.ShapeDtypeStruct((2 * nb - 1, 2 * cb, d), F32),
        compiler_params=_cparams(("arbitrary", "arbitrary")),
        name="hyena_filter_spectra",
    )(consts["f_fwd"], kap, kap, ksum, vec(d_bias))


def _hyena_consts(l, cb):
    pos_dim = 33
    bands = (pos_dim - 1) // 2
    t = jnp.linspace(0.0, 1.0, l, dtype=F32)[:, None]
    f = jnp.linspace(1e-4, bands - 1, bands, dtype=F32)
    w = 2.0 * math.pi * jnp.arange(l, dtype=F32)[:, None] / l
    z = jnp.concatenate([t, jnp.cos(f * w), -jnp.sin(f * w)], axis=-1)
    r = jnp.arange(2 * l)
    pos = jnp.abs(r - l)
    valid = (r > 0)
    pos = jnp.where(valid, pos, 0)
    zk = jnp.zeros((2 * l, POS_PAD), F32)
    zk = zk.at[:, :pos_dim].set(z[pos])
    zk = zk.at[:, DECAY_POS_COL].set(t[pos, 0])
    zk = zk.at[:, MASK_POS_COL].set(valid.astype(F32))
    k = jnp.arange(cb, dtype=jnp.int32)[:, None]
    n = jnp.arange(cb, dtype=jnp.int32)[None, :]
    m = ((2 * k + 1) * n) % (4 * cb)
    theta = m.astype(F32) * (math.pi / (2 * cb))
    cos, sin = jnp.cos(theta), jnp.sin(theta)
    f_fwd = jnp.concatenate([cos, -sin], axis=0)
    f_inv = jnp.concatenate([cos.T, -sin.T], axis=1) * (1.0 / cb)
    return {"zk": zk, "f_fwd": f_fwd.astype(BF16), "f_inv": f_inv.astype(BF16)}


def _hyena_in_kernel(xp_ref, x_ref, xn_ref, g_ref, w_ref, b_ref, cw_ref, cb_ref, vg_ref, x0_ref,
                     *, dh, dc):
    i = pl.program_id(1)
    last = pl.num_programs(1) - 1
    tm = x_ref.shape[1]
    rows = jnp.concatenate([xp_ref[0], x_ref[0], xn_ref[0]], axis=0)
    h = _rms(rows, g_ref[...]).astype(BF16)
    ridx = lax.broadcasted_iota(jnp.int32, (tm + 2 * SUBLANES, dc), 0)
    first_row = jnp.where(i == 0, SUBLANES, 0)
    end_row = jnp.where(i == last, tm + SUBLANES, tm + 2 * SUBLANES)
    keep = jnp.where((ridx >= first_row) & (ridx < end_row), 1.0, 0.0)

    def conv_part(col):
        u = (_dot(h, w_ref[:, col:col + dc]) + b_ref[:, col:col + dc]) * keep
        cw = cw_ref[:, col:col + dc]
        y = (pltpu.roll(u, 1, axis=0) * cw[0:1] + u * cw[1:2]
             + pltpu.roll(u, tm + 2 * SUBLANES - 1, axis=0) * cw[2:3])
        return y[SUBLANES:tm + SUBLANES] + cb_ref[:, col:col + dc]

    for c in range(dh // dc):
        x0 = conv_part(c * dc)
        x1 = conv_part(dh + c * dc)
        v = conv_part(2 * dh + c * dc)
        x0_ref[0, :, c * dc:(c + 1) * dc] = x0
        vg_ref[0, :, c * dc:(c + 1) * dc] = (v * x1).astype(BF16)


def _hyena_conv_kernel(v_ref, x0_ref, c_ref, ff_ref, fi_ref, z_ref, vh_ref, yh_ref, *, cb, nb, rc):
    for j in range(nb):
        vh_ref[j] = _dot(ff_ref[...], v_ref[0, j * cb:(j + 1) * cb, :])
    for i in range(nb):
        def chunk(r, carry, i=i):
            r0 = pl.multiple_of(r * rc, rc)
            yr = None
            for j in range(nb):
                dlt = i - j + nb - 1
                cr = c_ref[dlt, pl.ds(r0, rc), :]
                ci = c_ref[dlt, pl.ds(cb + r0, rc), :]
                vr = vh_ref[j, pl.ds(r0, rc), :]
                vi = vh_ref[j, pl.ds(cb + r0, rc), :]
                pr = cr * vr - ci * vi
                pi = cr * vi + ci * vr
                yr, yi = (pr, pi) if yr is None else (yr + pr, yi + pi)
            yh_ref[pl.ds(r0, rc), :] = yr.astype(BF16)
            yh_ref[pl.ds(cb + r0, rc), :] = yi.astype(BF16)
            return carry

        lax.fori_loop(0, cb // rc, chunk, 0)
        y = _dot(fi_ref[...], yh_ref[...])
        z_ref[0, i * cb:(i + 1) * cb, :] = (y * x0_ref[0, i * cb:(i + 1) * cb, :]).astype(BF16)


def _proj_res_kernel(z_ref, w_ref, b_ref, x_ref, o_ref):
    o_ref[...] = x_ref[...] + _dot(z_ref[...], w_ref[...]) + b_ref[...]


def _hyena_layer(x, g, w_in, b_in, conv_w, conv_b, spectra, consts, w_out, b_out, cb, tm=512,
                 dc=256, dt=256, rc=32):
    b, l, d = x.shape
    dh = w_out.shape[0]
    tm = min(tm, l)
    nb = l // cb
    hb = tm // SUBLANES
    nrow8 = l // SUBLANES
    in_kern = functools.partial(_hyena_in_kernel, dh=dh, dc=dc)
    const = lambda shape: _resident(shape, lambda bi, i: (0,) * len(shape))
    vg, x0 = pl.pallas_call(
        in_kern,
        grid=(b, l // tm),
        in_specs=[
            pl.BlockSpec((1, SUBLANES, d), lambda bi, i: (bi, jnp.maximum(i * hb - 1, 0), 0)),
            pl.BlockSpec((1, tm, d), lambda bi, i: (bi, i, 0)),
            pl.BlockSpec((1, SUBLANES, d), lambda bi, i: (bi, jnp.minimum((i + 1) * hb, nrow8 - 1), 0)),
            const((1, d)),
            const((d, 3 * dh)),
            const((1, 3 * dh)),
            const((SHORT_CONV, 3 * dh)),
            const((1, 3 * dh)),
        ],
        out_specs=[
            pl.BlockSpec((1, tm, dh), lambda bi, i: (bi, i, 0)),
            pl.BlockSpec((1, tm, dh), lambda bi, i: (bi, i, 0)),
        ],
        out_shape=[
            jax.ShapeDtypeStruct((b, l, dh), BF16),
            jax.ShapeDtypeStruct((b, l, dh), F32),
        ],
        compiler_params=_cparams(("arbitrary", "arbitrary")),
        name="hyena_in",
    )(x, x, x, g, w_in.astype(BF16), b_in[None, :], conv_w, conv_b[None, :])

    conv_kern = functools.partial(_hyena_conv_kernel, cb=cb, nb=nb, rc=rc)
    z = pl.pallas_call(
        conv_kern,
        grid=(dh // dt, b),
        in_specs=[
            pl.BlockSpec((1, l, dt), lambda j, bi: (bi, 0, j)),
            pl.BlockSpec((1, l, dt), lambda j, bi: (bi, 0, j)),
            _resident((2 * nb - 1, 2 * cb, dt), lambda j, bi: (0, 0, j)),
            _resident((2 * cb, cb), lambda j, bi: (0, 0)),
            _resident((cb, 2 * cb), lambda j, bi: (0, 0)),
        ],
        out_specs=pl.BlockSpec((1, l, dt), lambda j, bi: (bi, 0, j)),
        out_shape=jax.ShapeDtypeStruct((b, l, dh), BF16),
        scratch_shapes=[pltpu.VMEM((nb, 2 * cb, dt), F32), pltpu.VMEM((2 * cb, dt), BF16)],
        compiler_params=_cparams(("arbitrary", "arbitrary")),
        name="hyena_conv",
    )(vg, x0, spectra, consts["f_fwd"], consts["f_inv"])

    t = b * l
    tp = min(512, t)
    out = pl.pallas_call(
        _proj_res_kernel,
        grid=(t // tp,),
        in_specs=[
            pl.BlockSpec((tp, dh), lambda i: (i, 0)),
            _resident((dh, d), lambda i: (0, 0)),
            _resident((1, d), lambda i: (0, 0)),
            pl.BlockSpec((tp, d), lambda i: (i, 0)),
        ],
        out_specs=pl.BlockSpec((tp, d), lambda i: (i, 0)),
        out_shape=jax.ShapeDtypeStruct((t, d), F32),
        compiler_params=_cparams(("arbitrary",)),
        name="hyena_out",
    )(z.reshape(t, dh), w_out.astype(BF16), b_out[None, :], x.reshape(t, d))
    return out.reshape(b, l, d)


def kernel(x, ffn_norm_g, mix_norm_g, ffn_w_in, ffn_w_out, attn_w_in, attn_q_gain, attn_k_gain,
           attn_w_out, hy_w_in, hy_b_in, hy_conv_w, hy_conv_b, hy_f_w1, hy_f_b1, hy_f_w2, hy_f_b2,
           hy_f_w3, hy_f_b3, hy_f_freq, hy_f_w_out, hy_decay, hy_d_bias, hy_w_out, hy_b_out,
           final_norm_g):
    b, l, d = x.shape
    depth = mix_norm_g.shape[0]
    head_dim = attn_q_gain.shape[1]
    cb = min(CONV_BLOCK, l)
    attn_consts = _attention_consts(l, head_dim, N_Q_HEADS, N_KV_HEADS)
    hy_consts = _hyena_consts(l, cb)
    ffn_w_in = ffn_w_in.astype(BF16)
    ffn_w_out = ffn_w_out.astype(BF16)
    gf = final_norm_g[None, :]
    tm = min(512, b * l)

    def ffn(xx, i, k, final=False):
        y = _ffn(xx.reshape(b * l, d), ffn_norm_g[i, k][None, :], ffn_w_in[i, k], ffn_w_out[i, k],
                 gf, final, tm=tm)
        return y.reshape(b, l, d)

    for i in range(depth):
        x = ffn(x, i, 0)
        j = i // 2
        g = mix_norm_g[i][None, :]
        if i % 2 == 0:
            x = _attention_layer(x, g, attn_w_in[j], attn_q_gain[j], attn_k_gain[j], attn_w_out[j],
                                 attn_consts)
        else:
            spectra = _hyena_filter_spectra(hy_consts, hy_f_w1[j], hy_f_b1[j], hy_f_w2[j],
                                            hy_f_b2[j], hy_f_w3[j], hy_f_b3[j], hy_f_freq[j],
                                            hy_f_w_out[j], hy_decay[j], hy_d_bias[j], l, cb)
            x = _hyena_layer(x, g, hy_w_in[j], hy_b_in[j], hy_conv_w[j], hy_conv_b[j], spectra,
                             hy_consts, hy_w_out[j], hy_b_out[j], cb)
        x = ffn(x, i, 1, final=(i == depth - 1))
    return x
```

```python
import functools
import math

import jax
import jax.numpy as jnp
from jax import lax
from jax.experimental import pallas as pl
from jax.experimental.pallas import tpu as pltpu

F32 = jnp.float32
BF16 = jnp.bfloat16

NORM_EPS = 1e-6
FFN_RES_SCALE = 0.5
ROPE_THETA = 10000.0
GRID_W = 64
N_Q_HEADS = 16
N_KV_HEADS = 4
SHORT_CONV = 3
DECAY_POS_COL = 33
MASK_POS_COL = 34
POS_PAD = 40
CONV_BLOCK = 1024

LANES = 128
SUBLANES = 8
BF16_ROWS = 16
VMEM_LIMIT = 56 * 1024 * 1024

HIGHEST = lax.Precision.HIGHEST


def _cparams(sem):
    return pltpu.CompilerParams(dimension_semantics=sem, vmem_limit_bytes=VMEM_LIMIT)


def _resident(shape, index_map):
    return pl.BlockSpec(shape, index_map, pipeline_mode=pl.Buffered(1))


def _rms(x, g):
    ms = jnp.mean(x * x, axis=-1, keepdims=True)
    return x * lax.rsqrt(ms + NORM_EPS) * g


def _dot(a, b):
    return jnp.dot(a, b, preferred_element_type=F32)


def _ffn_kernel(x_ref, g_ref, win_ref, wout_ref, gf_ref, o_ref, acc_ref, *, d_ff, fc, final):
    x = x_ref[...]
    h = _rms(x, g_ref[...]).astype(BF16)
    for c in range(d_ff // fc):
        gate = _dot(h, win_ref[:, c * fc:(c + 1) * fc])
        up = _dot(h, win_ref[:, d_ff + c * fc:d_ff + (c + 1) * fc])
        a = (gate * jax.nn.sigmoid(gate) * up).astype(BF16)
        part = _dot(a, wout_ref[c * fc:(c + 1) * fc, :])
        if c == 0:
            acc_ref[...] = part
        else:
            acc_ref[...] += part
    y = x + FFN_RES_SCALE * acc_ref[...]
    if final:
        y = _rms(y, gf_ref[...])
    o_ref[...] = y


def _ffn(x2, g, w_in, w_out, gf, final, tm=512, fc=256):
    t, d = x2.shape
    d_ff = w_out.shape[0]
    kern = functools.partial(_ffn_kernel, d_ff=d_ff, fc=fc, final=final)
    return pl.pallas_call(
        kern,
        grid=(t // tm,),
        in_specs=[
            pl.BlockSpec((tm, d), lambda i: (i, 0)),
            _resident((1, d), lambda i: (0, 0)),
            _resident((d, 2 * d_ff), lambda i: (0, 0)),
            _resident((d_ff, d), lambda i: (0, 0)),
            _resident((1, d), lambda i: (0, 0)),
        ],
        out_specs=pl.BlockSpec((tm, d), lambda i: (i, 0)),
        out_shape=jax.ShapeDtypeStruct((t, d), F32),
        scratch_shapes=[pltpu.VMEM((tm, d), F32)],
        compiler_params=_cparams(("arbitrary",)),
        name="ffn",
    )(x2, g, w_in, w_out, gf)


def _head_rms_scale(q, gsum, gbc, head_dim):
    ss = _dot((q * q).astype(BF16), gsum)
    r = lax.rsqrt(ss * (1.0 / head_dim) + NORM_EPS)
    r_hi = r.astype(BF16)
    r_lo = (r - r_hi.astype(F32)).astype(BF16)
    return _dot(jnp.concatenate([r_hi, r_lo], axis=-1), gbc)


def _rope(xn, c_tab, so_tab, se_tab):
    w = xn.shape[-1]
    reps = w // LANES
    c = jnp.concatenate([c_tab] * reps, axis=-1)
    so = jnp.concatenate([so_tab] * reps, axis=-1)
    se = jnp.concatenate([se_tab] * reps, axis=-1)
    return xn * c + pltpu.roll(xn, 1, axis=1) * so + pltpu.roll(xn, w - 1, axis=1) * se


def _qkv_kernel(x_ref, g_ref, w_ref, qg_ref, kg_ref, gsq_ref, gbq_ref, gsk_ref, gbk_ref,
                cq_ref, soq_ref, seq_ref, ck_ref, sok_ref, sek_ref,
                qt_ref, k_ref, vt_ref, *, d_attn, d_kv, head_dim):
    h = _rms(x_ref[0], g_ref[...]).astype(BF16)
    qkv = _dot(h, w_ref[...])
    q = qkv[:, :d_attn]
    k = qkv[:, d_attn:d_attn + d_kv]
    v = qkv[:, d_attn + d_kv:]
    qn = q * _head_rms_scale(q, gsq_ref[...], gbq_ref[...], head_dim) * qg_ref[...]
    kn = k * _head_rms_scale(k, gsk_ref[...], gbk_ref[...], head_dim) * kg_ref[...]
    qr = _rope(qn, cq_ref[...], soq_ref[...], seq_ref[...])
    kr = _rope(kn, ck_ref[...], sok_ref[...], sek_ref[...])
    qt_ref[0] = qr.T.astype(BF16)
    vt = v.T
    tm = vt.shape[1]
    ones_row = jnp.where(lax.broadcasted_iota(jnp.int32, (BF16_ROWS, tm), 0) == 0, 1.0, 0.0)
    for gidx in range(d_kv // head_dim):
        k_ref[0, gidx] = kr[:, gidx * head_dim:(gidx + 1) * head_dim].astype(BF16)
        vt_ref[0, gidx, :head_dim, :] = vt[gidx * head_dim:(gidx + 1) * head_dim].astype(BF16)
        vt_ref[0, gidx, head_dim:, :] = ones_row.astype(BF16)


def _attn_kernel(x_ref, qt_ref, k_ref, vt_ref, wot_ref, o_ref, s0_ref, s1_ref, ot_ref, *, n_heads,
                 q_per_kv, head_dim, key_splits):
    n_keys = k_ref.shape[2]
    kc = n_keys // key_splits

    def scores(hd, s_ref):
        grp = hd // q_per_kv
        qt = qt_ref[0, hd]
        for c in range(key_splits):
            s_ref[c * kc:(c + 1) * kc, :] = _dot(k_ref[0, grp, c * kc:(c + 1) * kc, :], qt)

    def finish(hd, s_ref):
        grp = hd // q_per_kv
        m = jnp.max(s_ref[...], axis=0, keepdims=True)
        oa = None
        for c in range(key_splits):
            p = jnp.exp2(s_ref[c * kc:(c + 1) * kc, :] - m).astype(BF16)
            part = _dot(vt_ref[0, grp, :, c * kc:(c + 1) * kc], p)
            oa = part if oa is None else oa + part
        row = pl.multiple_of(hd * head_dim, head_dim)
        ot_ref[pl.ds(row, head_dim), :] = (oa[:head_dim] / oa[head_dim:head_dim + 1]).astype(BF16)

    def head_pair(i, carry):
        h0 = 2 * i
        scores(h0 + 1, s1_ref)
        finish(h0, s0_ref)
        scores(h0 + 2, s0_ref)
        finish(h0 + 1, s1_ref)
        return carry

    scores(0, s0_ref)
    lax.fori_loop(0, n_heads // 2 - 1, head_pair, 0)
    scores(n_heads - 1, s1_ref)
    finish(n_heads - 2, s0_ref)
    finish(n_heads - 1, s1_ref)
    yt = _dot(wot_ref[...], ot_ref[...])
    o_ref[0] = x_ref[0] + yt.T


def _attention_layer(x, g, w_in, q_gain, k_gain, w_out, consts, tm=512, tq=256, key_splits=2):
    b, l, d = x.shape
    head_dim = q_gain.shape[0]
    d_attn = w_out.shape[0]
    d_kv = (w_in.shape[1] - d_attn) // 2
    n_heads = d_attn // head_dim
    n_kv = d_kv // head_dim
    vrows = head_dim + BF16_ROWS
    tm = min(tm, l)
    tq = min(tq, l)
    tab = pl.BlockSpec((tm, LANES), lambda bi, i: (i, 0))
    const = lambda shape: _resident(shape, lambda bi, i: (0,) * len(shape))
    qkv_kern = functools.partial(_qkv_kernel, d_attn=d_attn, d_kv=d_kv, head_dim=head_dim)
    qt, k, vt = pl.pallas_call(
        qkv_kern,
        grid=(b, l // tm),
        in_specs=[
            pl.BlockSpec((1, tm, d), lambda bi, i: (bi, i, 0)),
            const((1, d)),
            const((d, d_attn + 2 * d_kv)),
            const((1, d_attn)),
            const((1, d_kv)),
            const((d_attn, LANES)),
            const((2 * LANES, d_attn)),
            const((d_kv, LANES)),
            const((2 * LANES, d_kv)),
            tab, tab, tab, tab, tab, tab,
        ],
        out_specs=[
            pl.BlockSpec((1, d_attn, tm), lambda bi, i: (bi, 0, i)),
            pl.BlockSpec((1, n_kv, tm, head_dim), lambda bi, i: (bi, 0, i, 0)),
            pl.BlockSpec((1, n_kv, vrows, tm), lambda bi, i: (bi, 0, 0, i)),
        ],
        out_shape=[
            jax.ShapeDtypeStruct((b, d_attn, l), BF16),
            jax.ShapeDtypeStruct((b, n_kv, l, head_dim), BF16),
            jax.ShapeDtypeStruct((b, n_kv, vrows, l), BF16),
        ],
        compiler_params=_cparams(("arbitrary", "arbitrary")),
        name="attn_qkv",
    )(x, g, w_in.astype(BF16),
      jnp.tile(q_gain, n_heads)[None, :], jnp.tile(k_gain, n_kv)[None, :],
      consts["gsum_q"], consts["gbc_q"], consts["gsum_k"], consts["gbc_k"],
      consts["cq"], consts["soq"], consts["seq"], consts["ck"], consts["sok"], consts["sek"])

    qt = qt.reshape(b, n_heads, head_dim, l)
    attn_kern = functools.partial(_attn_kernel, n_heads=n_heads, q_per_kv=n_heads // n_kv,
                                  head_dim=head_dim, key_splits=key_splits)
    return pl.pallas_call(
        attn_kern,
        grid=(b, l // tq),
        in_specs=[
            pl.BlockSpec((1, tq, d), lambda bi, i: (bi, i, 0)),
            pl.BlockSpec((1, n_heads, head_dim, tq), lambda bi, i: (bi, 0, 0, i)),
            pl.BlockSpec((1, n_kv, l, head_dim), lambda bi, i: (bi, 0, 0, 0)),
            pl.BlockSpec((1, n_kv, vrows, l), lambda bi, i: (bi, 0, 0, 0)),
            _resident((d, d_attn), lambda bi, i: (0, 0)),
        ],
        out_specs=pl.BlockSpec((1, tq, d), lambda bi, i: (bi, i, 0)),
        out_shape=jax.ShapeDtypeStruct((b, l, d), F32),
        scratch_shapes=[pltpu.VMEM((l, tq), F32), pltpu.VMEM((l, tq), F32),
                        pltpu.VMEM((d_attn, tq), BF16)],
        compiler_params=_cparams(("arbitrary", "arbitrary")),
        name="attn_core",
    )(x, qt, k, vt, w_out.T.astype(BF16))


def _attention_consts(l, head_dim, n_heads, n_kv):
    rows = l // GRID_W
    axis_dim = head_dim // 2
    row = jnp.repeat(jnp.arange(rows, dtype=F32), GRID_W)
    col = jnp.tile(jnp.arange(GRID_W, dtype=F32), rows)
    inv = ROPE_THETA ** (-jnp.arange(0, axis_dim, 2, dtype=F32) / axis_dim)
    ang = jnp.concatenate([row[:, None] * inv, col[:, None] * inv], axis=-1)
    cos = jnp.repeat(jnp.cos(ang), 2, axis=-1)
    sin = jnp.repeat(jnp.sin(ang), 2, axis=-1)
    odd = (jnp.arange(head_dim) % 2 == 1)[None, :]
    s_odd = jnp.where(odd, sin, 0.0)
    s_even = jnp.where(odd, 0.0, -sin)
    reps = LANES // head_dim
    wide = lambda a: jnp.tile(a, (1, reps))
    q_scale = (head_dim ** -0.5) * math.log2(math.e)
    out = {
        "cq": wide(cos) * q_scale, "soq": wide(s_odd) * q_scale, "seq": wide(s_even) * q_scale,
        "ck": wide(cos), "sok": wide(s_odd), "sek": wide(s_even),
    }
    for name, nh in (("q", n_heads), ("k", n_kv)):
        lane_head = jnp.arange(nh * head_dim) // head_dim
        gsum = (lane_head[:, None] == jnp.arange(LANES)[None, :]).astype(BF16)
        out["gsum_" + name] = gsum
        out["gbc_" + name] = jnp.concatenate([gsum.T, gsum.T], axis=0)
    return out


def _filter_taps_kernel(zk_ref, w1_ref, b1_ref, w2_ref, b2_ref, w3_ref, b3_ref, fr_ref, wo_ref,
                        dec_ref, kap_ref, sum_ref):
    zk = zk_ref[...]
    fr = fr_ref[...]
    hid = jnp.sin(fr * (jnp.dot(zk, w1_ref[...], precision=HIGHEST, preferred_element_type=F32)
                        + b1_ref[...]))
    hid = jnp.sin(fr * (jnp.dot(hid, w2_ref[...], precision=HIGHEST, preferred_element_type=F32)
                        + b2_ref[...]))
    hid = jnp.sin(fr * (jnp.dot(hid, w3_ref[...], precision=HIGHEST, preferred_element_type=F32)
                        + b3_ref[...]))
    filt = jnp.dot(hid, wo_ref[...], precision=HIGHEST, preferred_element_type=F32)
    t = zk[:, DECAY_POS_COL:DECAY_POS_COL + 1]
    mask = zk[:, MASK_POS_COL:MASK_POS_COL + 1]
    kap = filt * jnp.exp(-t * jnp.abs(dec_ref[0])) * mask
    kap_ref[...] = kap

    @pl.when(pl.program_id(0) == 0)
    def _():
        sum_ref[...] = jnp.zeros_like(sum_ref)

    sum_ref[...] += jnp.sum(jnp.abs(kap), axis=0, keepdims=True)


def _filter_spec_kernel(f_ref, klo_ref, khi_ref, sum_ref, dbias_ref, c_ref, *, cb, nb):
    dp = pl.program_id(0)
    inv_sum = 1.0 / sum_ref[...]
    dbias = dbias_ref[...]

    def spectrum(k_ref, is_zero_lag_block):
        kap = k_ref[...]
        hi = kap.astype(BF16)
        lo = (kap - hi.astype(F32)).astype(BF16)
        a = (_dot(f_ref[...], hi) + _dot(f_ref[...], lo)) * inv_sum
        return a[:cb] + jnp.where(is_zero_lag_block, dbias, 0.0), a[cb:]

    ar1, ai1 = spectrum(khi_ref, dp + 1 == nb)
    ar0, ai0 = spectrum(klo_ref, dp == nb)
    kidx = lax.broadcasted_iota(jnp.int32, ar1.shape, 0)
    sgn = (1 - 2 * (kidx % 2)).astype(F32)
    c_ref[0, :cb, :] = ar1 - sgn * ai0
    c_ref[0, cb:, :] = ai1 + sgn * ar0


def _hyena_filter_spectra(consts, w1, b1, w2, b2, w3, b3, freq, w_out, decay, d_bias, l, cb,
                          tr=512, dtile=512):
    d = d_bias.shape[0]
    nb = l // cb
    tr = min(tr, l)
    hidden = w2.shape[0]
    w1p = jnp.zeros((POS_PAD, hidden), F32).at[:w1.shape[0]].set(w1)
    half_steps = l // tr
    vec = lambda a: a[None, :]
    const = lambda shape: _resident(shape, lambda i: (0,) * len(shape))
    kap, ksum = pl.pallas_call(
        _filter_taps_kernel,
        grid=(2 * l // tr,),
        in_specs=[
            pl.BlockSpec((tr, POS_PAD), lambda i: (i, 0)),
            const((POS_PAD, hidden)), const((1, hidden)),
            const((hidden, hidden)), const((1, hidden)),
            const((hidden, hidden)), const((1, hidden)),
            const((1, hidden)),
            pl.BlockSpec((hidden, d), lambda i: (0, jnp.where(i < half_steps, 1, 0))),
            pl.BlockSpec((1, 1, d), lambda i: (jnp.where(i < half_steps, 1, 0), 0, 0)),
        ],
        out_specs=[
            pl.BlockSpec((tr, d), lambda i: (i, 0)),
            pl.BlockSpec((1, d), lambda i: (0, 0)),
        ],
        out_shape=[
            jax.ShapeDtypeStruct((2 * l, d), F32),
            jax.ShapeDtypeStruct((1, d), F32),
        ],
        compiler_params=_cparams(("arbitrary",)),
        name="hyena_filter_taps",
    )(consts["zk"], w1p, vec(b1), w2, vec(b2), w3, vec(b3), vec(freq), w_out,
      decay[:, None, :])

    dtile = min(dtile, d)
    spec_kern = functools.partial(_filter_spec_kernel, cb=cb, nb=nb)
    return pl.pallas_call(
        spec_kern,
        grid=(2 * nb - 1, d // dtile),
        in_specs=[
            _resident((2 * cb, cb), lambda dp, j: (0, 0)),
            pl.BlockSpec((cb, dtile), lambda dp, j: (dp, j)),
            pl.BlockSpec((cb, dtile), lambda dp, j: (dp + 1, j)),
            pl.BlockSpec((1, dtile), lambda dp, j: (0, j)),
            pl.BlockSpec((1, dtile), lambda dp, j: (0, j)),
        ],
        out_specs=pl.BlockSpec((1, 2 * cb, dtile), lambda dp, j: (dp, 0, j)),
        out_shape=jax.ShapeDtypeStruct((2 * nb - 1, 2 * cb, d), F32),
        compiler_params=_cparams(("arbitrary", "arbitrary")),
        name="hyena_filter_spectra",
    )(consts["f_fwd"], kap, kap, ksum, vec(d_bias))


def _hyena_consts(l, cb):
    pos_dim = 33
    bands = (pos_dim - 1) // 2
    t = jnp.linspace(0.0, 1.0, l, dtype=F32)[:, None]
    f = jnp.linspace(1e-4, bands - 1, bands, dtype=F32)
    w = 2.0 * math.pi * jnp.arange(l, dtype=F32)[:, None] / l
    z = jnp.concatenate([t, jnp.cos(f * w), -jnp.sin(f * w)], axis=-1)
    r = jnp.arange(2 * l)
    pos = jnp.abs(r - l)
    valid = (r > 0)
    pos = jnp.where(valid, pos, 0)
    zk = jnp.zeros((2 * l, POS_PAD), F32)
    zk = zk.at[:, :pos_dim].set(z[pos])
    zk = zk.at[:, DECAY_POS_COL].set(t[pos, 0])
    zk = zk.at[:, MASK_POS_COL].set(valid.astype(F32))
    k = jnp.arange(cb, dtype=jnp.int32)[:, None]
    n = jnp.arange(cb, dtype=jnp.int32)[None, :]
    m = ((2 * k + 1) * n) % (4 * cb)
    theta = m.astype(F32) * (math.pi / (2 * cb))
    cos, sin = jnp.cos(theta), jnp.sin(theta)
    f_fwd = jnp.concatenate([cos, -sin], axis=0)
    f_inv = jnp.concatenate([cos.T, -sin.T], axis=1) * (1.0 / cb)
    return {"zk": zk, "f_fwd": f_fwd.astype(BF16), "f_inv": f_inv.astype(BF16)}


def _hyena_in_kernel(xp_ref, x_ref, xn_ref, g_ref, w_ref, b_ref, cw_ref, cb_ref, vg_ref, x0_ref,
                     *, dh, dc):
    i = pl.program_id(1)
    last = pl.num_programs(1) - 1
    tm = x_ref.shape[1]
    rows = jnp.concatenate([xp_ref[0], x_ref[0], xn_ref[0]], axis=0)
    h = _rms(rows, g_ref[...]).astype(BF16)
    ridx = lax.broadcasted_iota(jnp.int32, (tm + 2 * SUBLANES, dc), 0)
    first_row = jnp.where(i == 0, SUBLANES, 0)
    end_row = jnp.where(i == last, tm + SUBLANES, tm + 2 * SUBLANES)
    keep = jnp.where((ridx >= first_row) & (ridx < end_row), 1.0, 0.0)

    def conv_part(col):
        u = (_dot(h, w_ref[:, col:col + dc]) + b_ref[:, col:col + dc]) * keep
        cw = cw_ref[:, col:col + dc]
        y = (pltpu.roll(u, 1, axis=0) * cw[0:1] + u * cw[1:2]
             + pltpu.roll(u, tm + 2 * SUBLANES - 1, axis=0) * cw[2:3])
        return y[SUBLANES:tm + SUBLANES] + cb_ref[:, col:col + dc]

    for c in range(dh // dc):
        x0 = conv_part(c * dc)
        x1 = conv_part(dh + c * dc)
        v = conv_part(2 * dh + c * dc)
        x0_ref[0, :, c * dc:(c + 1) * dc] = x0
        vg_ref[0, :, c * dc:(c + 1) * dc] = (v * x1).astype(BF16)


def _hyena_conv_kernel(v_ref, x0_ref, c_ref, ff_ref, fi_ref, z_ref, vh_ref, yh_ref, *, cb, nb, rc):
    for j in range(nb):
        vh_ref[j] = _dot(ff_ref[...], v_ref[0, j * cb:(j + 1) * cb, :])
    for i in range(nb):
        for r0 in range(0, cb, rc):
            yr = None
            for j in range(nb):
                dlt = i - j + nb - 1
                cr = c_ref[dlt, r0:r0 + rc, :]
                ci = c_ref[dlt, cb + r0:cb + r0 + rc, :]
                vr = vh_ref[j, r0:r0 + rc, :]
                vi = vh_ref[j, cb + r0:cb + r0 + rc, :]
                pr = cr * vr - ci * vi
                pi = cr * vi + ci * vr
                yr, yi = (pr, pi) if yr is None else (yr + pr, yi + pi)
            yh_ref[i, r0:r0 + rc, :] = yr.astype(BF16)
            yh_ref[i, cb + r0:cb + r0 + rc, :] = yi.astype(BF16)
        y = _dot(fi_ref[...], yh_ref[i])
        z_ref[0, i * cb:(i + 1) * cb, :] = (y * x0_ref[0, i * cb:(i + 1) * cb, :]).astype(BF16)


def _proj_res_kernel(z_ref, w_ref, b_ref, x_ref, o_ref):
    o_ref[...] = x_ref[...] + _dot(z_ref[...], w_ref[...]) + b_ref[...]


def _hyena_layer(x, g, w_in, b_in, conv_w, conv_b, spectra, consts, w_out, b_out, cb, tm=512,
                 dc=256, dt=256, rc=16):
    b, l, d = x.shape
    dh = w_out.shape[0]
    tm = min(tm, l)
    nb = l // cb
    hb = tm // SUBLANES
    nrow8 = l // SUBLANES
    in_kern = functools.partial(_hyena_in_kernel, dh=dh, dc=dc)
    const = lambda shape: _resident(shape, lambda bi, i: (0,) * len(shape))
    vg, x0 = pl.pallas_call(
        in_kern,
        grid=(b, l // tm),
        in_specs=[
            pl.BlockSpec((1, SUBLANES, d), lambda bi, i: (bi, jnp.maximum(i * hb - 1, 0), 0)),
            pl.BlockSpec((1, tm, d), lambda bi, i: (bi, i, 0)),
            pl.BlockSpec((1, SUBLANES, d), lambda bi, i: (bi, jnp.minimum((i + 1) * hb, nrow8 - 1), 0)),
            const((1, d)),
            const((d, 3 * dh)),
            const((1, 3 * dh)),
            const((SHORT_CONV, 3 * dh)),
            const((1, 3 * dh)),
        ],
        out_specs=[
            pl.BlockSpec((1, tm, dh), lambda bi, i: (bi, i, 0)),
            pl.BlockSpec((1, tm, dh), lambda bi, i: (bi, i, 0)),
        ],
        out_shape=[
            jax.ShapeDtypeStruct((b, l, dh), BF16),
            jax.ShapeDtypeStruct((b, l, dh), F32),
        ],
        compiler_params=_cparams(("arbitrary", "arbitrary")),
        name="hyena_in",
    )(x, x, x, g, w_in.astype(BF16), b_in[None, :], conv_w, conv_b[None, :])

    conv_kern = functools.partial(_hyena_conv_kernel, cb=cb, nb=nb, rc=rc)
    z = pl.pallas_call(
        conv_kern,
        grid=(dh // dt, b),
        in_specs=[
            pl.BlockSpec((1, l, dt), lambda j, bi: (bi, 0, j)),
            pl.BlockSpec((1, l, dt), lambda j, bi: (bi, 0, j)),
            _resident((2 * nb - 1, 2 * cb, dt), lambda j, bi: (0, 0, j)),
            _resident((2 * cb, cb), lambda j, bi: (0, 0)),
            _resident((cb, 2 * cb), lambda j, bi: (0, 0)),
        ],
        out_specs=pl.BlockSpec((1, l, dt), lambda j, bi: (bi, 0, j)),
        out_shape=jax.ShapeDtypeStruct((b, l, dh), BF16),
        scratch_shapes=[pltpu.VMEM((nb, 2 * cb, dt), F32), pltpu.VMEM((nb, 2 * cb, dt), BF16)],
        compiler_params=_cparams(("arbitrary", "arbitrary")),
        name="hyena_conv",
    )(vg, x0, spectra, consts["f_fwd"], consts["f_inv"])

    t = b * l
    tp = min(512, t)
    out = pl.pallas_call(
        _proj_res_kernel,
        grid=(t // tp,),
        in_specs=[
            pl.BlockSpec((tp, dh), lambda i: (i, 0)),
            _resident((dh, d), lambda i: (0, 0)),
            _resident((1, d), lambda i: (0, 0)),
            pl.BlockSpec((tp, d), lambda i: (i, 0)),
        ],
        out_specs=pl.BlockSpec((tp, d), lambda i: (i, 0)),
        out_shape=jax.ShapeDtypeStruct((t, d), F32),
        compiler_params=_cparams(("arbitrary",)),
        name="hyena_out",
    )(z.reshape(t, dh), w_out.astype(BF16), b_out[None, :], x.reshape(t, d))
    return out.reshape(b, l, d)


def kernel(x, ffn_norm_g, mix_norm_g, ffn_w_in, ffn_w_out, attn_w_in, attn_q_gain, attn_k_gain,
           attn_w_out, hy_w_in, hy_b_in, hy_conv_w, hy_conv_b, hy_f_w1, hy_f_b1, hy_f_w2, hy_f_b2,
           hy_f_w3, hy_f_b3, hy_f_freq, hy_f_w_out, hy_decay, hy_d_bias, hy_w_out, hy_b_out,
           final_norm_g):
    b, l, d = x.shape
    depth = mix_norm_g.shape[0]
    head_dim = attn_q_gain.shape[1]
    cb = min(CONV_BLOCK, l)
    attn_consts = _attention_consts(l, head_dim, N_Q_HEADS, N_KV_HEADS)
    hy_consts = _hyena_consts(l, cb)
    ffn_w_in = ffn_w_in.astype(BF16)
    ffn_w_out = ffn_w_out.astype(BF16)
    gf = final_norm_g[None, :]
    tm = min(512, b * l)

    def ffn(xx, i, k, final=False):
        y = _ffn(xx.reshape(b * l, d), ffn_norm_g[i, k][None, :], ffn_w_in[i, k], ffn_w_out[i, k],
                 gf, final, tm=tm)
        return y.reshape(b, l, d)

    for i in range(depth):
        x = ffn(x, i, 0)
        j = i // 2
        g = mix_norm_g[i][None, :]
        if i % 2 == 0:
            x = _attention_layer(x, g, attn_w_in[j], attn_q_gain[j], attn_k_gain[j], attn_w_out[j],
                                 attn_consts)
        else:
            spectra = _hyena_filter_spectra(hy_consts, hy_f_w1[j], hy_f_b1[j], hy_f_w2[j],
                                            hy_f_b2[j], hy_f_w3[j], hy_f_b3[j], hy_f_freq[j],
                                            hy_f_w_out[j], hy_decay[j], hy_d_bias[j], l, cb)
            x = _hyena_layer(x, g, hy_w_in[j], hy_b_in[j], hy_conv_w[j], hy_conv_b[j], spectra,
                             hy_consts, hy_w_out[j], hy_b_out[j], cb)
        x = ffn(x, i, 1, final=(i == depth - 1))
    return x
```

```python
import functools
import math

import jax
import jax.numpy as jnp
import numpy as np
from jax import lax
from jax.experimental import pallas as pl
from jax.experimental.pallas import tpu as pltpu

F32 = jnp.float32
BF16 = jnp.bfloat16

NORM_EPS = 1e-6
FFN_RES_SCALE = 0.5
ROPE_THETA = 10000.0
GRID_W = 64
N_Q_HEADS = 16
N_KV_HEADS = 4
SHORT_CONV = 3
DECAY_POS_COL = 33
MASK_POS_COL = 34
POS_PAD = 40
CONV_BLOCK = 1024

LANES = 128
SUBLANES = 8
BF16_ROWS = 16
VMEM_LIMIT = 56 * 1024 * 1024

HIGHEST = lax.Precision.HIGHEST


def _cparams(sem):
    return pltpu.CompilerParams(dimension_semantics=sem, vmem_limit_bytes=VMEM_LIMIT)


def _resident(shape, index_map):
    return pl.BlockSpec(shape, index_map, pipeline_mode=pl.Buffered(1))


def _rms(x, g):
    ms = jnp.mean(x * x, axis=-1, keepdims=True)
    return x * lax.rsqrt(ms + NORM_EPS) * g


def _dot(a, b):
    return jnp.dot(a, b, preferred_element_type=F32)


def _ffn_kernel(*refs, d_ff, fc, final, pre_proj):
    if pre_proj:
        z_ref, wo_ref, bo_ref, x_ref, g_ref, win_ref, wout_ref, gf_ref, o_ref, acc_ref = refs
        x = x_ref[...] + _dot(z_ref[...], wo_ref[...]) + bo_ref[...]
    else:
        x_ref, g_ref, win_ref, wout_ref, gf_ref, o_ref, acc_ref = refs
        x = x_ref[...]
    h = _rms(x, g_ref[...]).astype(BF16)
    for c in range(d_ff // fc):
        gate = _dot(h, win_ref[:, c * fc:(c + 1) * fc])
        up = _dot(h, win_ref[:, d_ff + c * fc:d_ff + (c + 1) * fc])
        a = (gate * jax.nn.sigmoid(gate) * up).astype(BF16)
        part = _dot(a, wout_ref[c * fc:(c + 1) * fc, :])
        if c == 0:
            acc_ref[...] = part
        else:
            acc_ref[...] += part
    y = x + FFN_RES_SCALE * acc_ref[...]
    if final:
        y = _rms(y, gf_ref[...])
    o_ref[...] = y


def _ffn(x2, g, w_in_all, w_out_all, layer, k, gf, final, pre=None, tm=512, fc=256):
    t, d = x2.shape
    d_ff = w_out_all.shape[2]
    kern = functools.partial(_ffn_kernel, d_ff=d_ff, fc=fc, final=final, pre_proj=pre is not None)
    row_tile = lambda width: pl.BlockSpec((tm, width), lambda i: (i, 0))
    const = lambda shape: _resident(shape, lambda i: (0,) * len(shape))
    in_specs = [
        row_tile(d),
        const((1, d)),
        _resident((None, None, d, 2 * d_ff), lambda i: (layer, k, 0, 0)),
        _resident((None, None, d_ff, d), lambda i: (layer, k, 0, 0)),
        const((1, d)),
    ]
    args = [x2, g, w_in_all, w_out_all, gf]
    if pre is not None:
        z2, w_o, b_o = pre
        in_specs = [row_tile(z2.shape[1]), const(w_o.shape), const((1, d))] + in_specs
        args = [z2, w_o, b_o] + args
    return pl.pallas_call(
        kern,
        grid=(t // tm,),
        in_specs=in_specs,
        out_specs=row_tile(d),
        out_shape=jax.ShapeDtypeStruct((t, d), F32),
        scratch_shapes=[pltpu.VMEM((tm, d), F32)],
        compiler_params=_cparams(("arbitrary",)),
        name="ffn",
    )(*args)


def _head_rms_scale(q, gsum, gbc, head_dim):
    ss = _dot((q * q).astype(BF16), gsum)
    r = lax.rsqrt(ss * (1.0 / head_dim) + NORM_EPS)
    r_hi = r.astype(BF16)
    r_lo = (r - r_hi.astype(F32)).astype(BF16)
    return _dot(jnp.concatenate([r_hi, r_lo], axis=-1), gbc)


def _rope(xn, c_tab, so_tab, se_tab):
    w = xn.shape[-1]
    reps = w // LANES
    c = jnp.concatenate([c_tab] * reps, axis=-1)
    so = jnp.concatenate([so_tab] * reps, axis=-1)
    se = jnp.concatenate([se_tab] * reps, axis=-1)
    return xn * c + pltpu.roll(xn, 1, axis=1) * so + pltpu.roll(xn, w - 1, axis=1) * se


def _qkv_kernel(x_ref, g_ref, w_ref, qg_ref, kg_ref, gsq_ref, gbq_ref, gsk_ref, gbk_ref,
                cq_ref, soq_ref, seq_ref, ck_ref, sok_ref, sek_ref,
                qt_ref, k_ref, vt_ref, *, d_attn, d_kv, head_dim):
    h = _rms(x_ref[0], g_ref[...]).astype(BF16)
    qkv = _dot(h, w_ref[...])
    q = qkv[:, :d_attn]
    k = qkv[:, d_attn:d_attn + d_kv]
    v = qkv[:, d_attn + d_kv:]
    qn = q * _head_rms_scale(q, gsq_ref[...], gbq_ref[...], head_dim) * qg_ref[...]
    kn = k * _head_rms_scale(k, gsk_ref[...], gbk_ref[...], head_dim) * kg_ref[...]
    qr = _rope(qn, cq_ref[...], soq_ref[...], seq_ref[...])
    kr = _rope(kn, ck_ref[...], sok_ref[...], sek_ref[...])
    qt_ref[0] = qr.T.astype(BF16)
    vt = v.T
    tm = vt.shape[1]
    ones_row = jnp.where(lax.broadcasted_iota(jnp.int32, (BF16_ROWS, tm), 0) == 0, 1.0, 0.0)
    for gidx in range(d_kv // head_dim):
        k_ref[0, gidx] = kr[:, gidx * head_dim:(gidx + 1) * head_dim].astype(BF16)
        vt_ref[0, gidx, :head_dim, :] = vt[gidx * head_dim:(gidx + 1) * head_dim].astype(BF16)
        vt_ref[0, gidx, head_dim:, :] = ones_row.astype(BF16)


def _attn_kernel(x_ref, qt_ref, k_ref, vt_ref, wot_ref, o_ref, s0_ref, s1_ref, m0_ref, m1_ref,
                 ot_ref, *, n_heads, q_per_kv, head_dim, key_splits):
    n_keys = k_ref.shape[2]
    kc = n_keys // key_splits

    def scores(hd, s_ref, m_ref):
        grp = hd // q_per_kv
        qt = qt_ref[0, hd]
        m = None
        for c in range(key_splits):
            sv = _dot(k_ref[0, grp, c * kc:(c + 1) * kc, :], qt)
            s_ref[c * kc:(c + 1) * kc, :] = sv
            mc = jnp.max(sv, axis=0, keepdims=True)
            m = mc if m is None else jnp.maximum(m, mc)
        m_ref[...] = m

    def finish(hd, s_ref, m_ref):
        grp = hd // q_per_kv
        m = m_ref[...]
        oa = None
        for c in range(key_splits):
            p = jnp.exp2(s_ref[c * kc:(c + 1) * kc, :] - m).astype(BF16)
            part = _dot(vt_ref[0, grp, :, c * kc:(c + 1) * kc], p)
            oa = part if oa is None else oa + part
        row = pl.multiple_of(hd * head_dim, head_dim)
        ot_ref[pl.ds(row, head_dim), :] = (oa[:head_dim] / oa[head_dim:head_dim + 1]).astype(BF16)

    def head_pair(i, carry):
        h0 = 2 * i
        scores(h0 + 1, s1_ref, m1_ref)
        finish(h0, s0_ref, m0_ref)
        scores(h0 + 2, s0_ref, m0_ref)
        finish(h0 + 1, s1_ref, m1_ref)
        return carry

    scores(0, s0_ref, m0_ref)
    lax.fori_loop(0, n_heads // 2 - 1, head_pair, 0)
    scores(n_heads - 1, s1_ref, m1_ref)
    finish(n_heads - 2, s0_ref, m0_ref)
    finish(n_heads - 1, s1_ref, m1_ref)
    yt = _dot(wot_ref[...], ot_ref[...])
    o_ref[0] = x_ref[0] + yt.T


def _attention_layer(x, g, w_in, q_gain, k_gain, w_out, consts, tm=512, tq=256, key_splits=2):
    b, l, d = x.shape
    head_dim = q_gain.shape[0]
    d_attn = w_out.shape[0]
    d_kv = (w_in.shape[1] - d_attn) // 2
    n_heads = d_attn // head_dim
    n_kv = d_kv // head_dim
    vrows = head_dim + BF16_ROWS
    tm = min(tm, l)
    tq = min(tq, l)
    tab = pl.BlockSpec((tm, LANES), lambda bi, i: (i, 0))
    const = lambda shape: _resident(shape, lambda bi, i: (0,) * len(shape))
    qkv_kern = functools.partial(_qkv_kernel, d_attn=d_attn, d_kv=d_kv, head_dim=head_dim)
    qt, k, vt = pl.pallas_call(
        qkv_kern,
        grid=(b, l // tm),
        in_specs=[
            pl.BlockSpec((1, tm, d), lambda bi, i: (bi, i, 0)),
            const((1, d)),
            const((d, d_attn + 2 * d_kv)),
            const((1, d_attn)),
            const((1, d_kv)),
            const((d_attn, LANES)),
            const((2 * LANES, d_attn)),
            const((d_kv, LANES)),
            const((2 * LANES, d_kv)),
            tab, tab, tab, tab, tab, tab,
        ],
        out_specs=[
            pl.BlockSpec((1, d_attn, tm), lambda bi, i: (bi, 0, i)),
            pl.BlockSpec((1, n_kv, tm, head_dim), lambda bi, i: (bi, 0, i, 0)),
            pl.BlockSpec((1, n_kv, vrows, tm), lambda bi, i: (bi, 0, 0, i)),
        ],
        out_shape=[
            jax.ShapeDtypeStruct((b, d_attn, l), BF16),
            jax.ShapeDtypeStruct((b, n_kv, l, head_dim), BF16),
            jax.ShapeDtypeStruct((b, n_kv, vrows, l), BF16),
        ],
        compiler_params=_cparams(("arbitrary", "arbitrary")),
        name="attn_qkv",
    )(x, g, w_in.astype(BF16),
      jnp.tile(q_gain, n_heads)[None, :], jnp.tile(k_gain, n_kv)[None, :],
      consts["gsum_q"], consts["gbc_q"], consts["gsum_k"], consts["gbc_k"],
      consts["cq"], consts["soq"], consts["seq"], consts["ck"], consts["sok"], consts["sek"])

    qt = qt.reshape(b, n_heads, head_dim, l)
    attn_kern = functools.partial(_attn_kernel, n_heads=n_heads, q_per_kv=n_heads // n_kv,
                                  head_dim=head_dim, key_splits=key_splits)
    return pl.pallas_call(
        attn_kern,
        grid=(b, l // tq),
        in_specs=[
            pl.BlockSpec((1, tq, d), lambda bi, i: (bi, i, 0)),
            pl.BlockSpec((1, n_heads, head_dim, tq), lambda bi, i: (bi, 0, 0, i)),
            pl.BlockSpec((1, n_kv, l, head_dim), lambda bi, i: (bi, 0, 0, 0)),
            pl.BlockSpec((1, n_kv, vrows, l), lambda bi, i: (bi, 0, 0, 0)),
            _resident((d, d_attn), lambda bi, i: (0, 0)),
        ],
        out_specs=pl.BlockSpec((1, tq, d), lambda bi, i: (bi, i, 0)),
        out_shape=jax.ShapeDtypeStruct((b, l, d), F32),
        scratch_shapes=[pltpu.VMEM((l, tq), F32), pltpu.VMEM((l, tq), F32),
                        pltpu.VMEM((1, tq), F32), pltpu.VMEM((1, tq), F32),
                        pltpu.VMEM((d_attn, tq), BF16)],
        compiler_params=_cparams(("arbitrary", "arbitrary")),
        name="attn_core",
    )(x, qt, k, vt, w_out.T.astype(BF16))


def _attention_consts(l, head_dim, n_heads, n_kv):
    rows = l // GRID_W
    axis_dim = head_dim // 2
    row = jnp.repeat(jnp.arange(rows, dtype=F32), GRID_W)
    col = jnp.tile(jnp.arange(GRID_W, dtype=F32), rows)
    inv = ROPE_THETA ** (-jnp.arange(0, axis_dim, 2, dtype=F32) / axis_dim)
    ang = jnp.concatenate([row[:, None] * inv, col[:, None] * inv], axis=-1)
    cos = jnp.repeat(jnp.cos(ang), 2, axis=-1)
    sin = jnp.repeat(jnp.sin(ang), 2, axis=-1)
    odd = (jnp.arange(head_dim) % 2 == 1)[None, :]
    s_odd = jnp.where(odd, sin, 0.0)
    s_even = jnp.where(odd, 0.0, -sin)
    reps = LANES // head_dim
    wide = lambda a: jnp.tile(a, (1, reps))
    q_scale = (head_dim ** -0.5) * math.log2(math.e)
    out = {
        "cq": wide(cos) * q_scale, "soq": wide(s_odd) * q_scale, "seq": wide(s_even) * q_scale,
        "ck": wide(cos), "sok": wide(s_odd), "sek": wide(s_even),
    }
    for name, nh in (("q", n_heads), ("k", n_kv)):
        lane_head = jnp.arange(nh * head_dim) // head_dim
        gsum = (lane_head[:, None] == jnp.arange(LANES)[None, :]).astype(BF16)
        out["gsum_" + name] = gsum
        out["gbc_" + name] = jnp.concatenate([gsum.T, gsum.T], axis=0)
    return out


def _filter_taps_kernel(zk_ref, w1_ref, b1_ref, w2_ref, b2_ref, w3_ref, b3_ref, fr_ref, wo_ref,
                        dec_ref, kap_ref, sum_ref):
    zk = zk_ref[...]
    fr = fr_ref[...]
    hid = jnp.sin(fr * (jnp.dot(zk, w1_ref[...], precision=HIGHEST, preferred_element_type=F32)
                        + b1_ref[...]))
    hid = jnp.sin(fr * (jnp.dot(hid, w2_ref[...], precision=HIGHEST, preferred_element_type=F32)
                        + b2_ref[...]))
    hid = jnp.sin(fr * (jnp.dot(hid, w3_ref[...], precision=HIGHEST, preferred_element_type=F32)
                        + b3_ref[...]))
    filt = jnp.dot(hid, wo_ref[...], precision=HIGHEST, preferred_element_type=F32)
    t = zk[:, DECAY_POS_COL:DECAY_POS_COL + 1]
    mask = zk[:, MASK_POS_COL:MASK_POS_COL + 1]
    kap = filt * jnp.exp(-t * jnp.abs(dec_ref[0])) * mask
    kap_ref[...] = kap

    @pl.when(pl.program_id(0) == 0)
    def _():
        sum_ref[...] = jnp.zeros_like(sum_ref)

    sum_ref[...] += jnp.sum(jnp.abs(kap), axis=0, keepdims=True)


def _filter_spec_kernel(f_ref, klo_ref, khi_ref, sum_ref, dbias_ref, c_ref, *, cb, nb):
    dp = pl.program_id(0)
    inv_sum = 1.0 / sum_ref[...]
    dbias = dbias_ref[...]

    def spectrum(k_ref, is_zero_lag_block):
        kap = k_ref[...]
        hi = kap.astype(BF16)
        lo = (kap - hi.astype(F32)).astype(BF16)
        a = (_dot(f_ref[...], hi) + _dot(f_ref[...], lo)) * inv_sum
        return a[:cb] + jnp.where(is_zero_lag_block, dbias, 0.0), a[cb:]

    ar1, ai1 = spectrum(khi_ref, dp + 1 == nb)
    ar0, ai0 = spectrum(klo_ref, dp == nb)
    kidx = lax.broadcasted_iota(jnp.int32, ar1.shape, 0)
    sgn = (1 - 2 * (kidx % 2)).astype(F32)
    c_ref[0, :cb, :] = ar1 - sgn * ai0
    c_ref[0, cb:, :] = ai1 + sgn * ar0


def _hyena_filter_spectra(consts, w1, b1, w2, b2, w3, b3, freq, w_out, decay, d_bias, l, cb,
                          tr=512, dtile=512):
    d = d_bias.shape[0]
    nb = l // cb
    tr = min(tr, l)
    hidden = w2.shape[0]
    w1p = jnp.zeros((POS_PAD, hidden), F32).at[:w1.shape[0]].set(w1)
    half_steps = l // tr
    vec = lambda a: a[None, :]
    const = lambda shape: _resident(shape, lambda i: (0,) * len(shape))
    kap, ksum = pl.pallas_call(
        _filter_taps_kernel,
        grid=(2 * l // tr,),
        in_specs=[
            pl.BlockSpec((tr, POS_PAD), lambda i: (i, 0)),
            const((POS_PAD, hidden)), const((1, hidden)),
            const((hidden, hidden)), const((1, hidden)),
            const((hidden, hidden)), const((1, hidden)),
            const((1, hidden)),
            pl.BlockSpec((hidden, d), lambda i: (0, jnp.where(i < half_steps, 1, 0))),
            pl.BlockSpec((1, 1, d), lambda i: (jnp.where(i < half_steps, 1, 0), 0, 0)),
        ],
        out_specs=[
            pl.BlockSpec((tr, d), lambda i: (i, 0)),
            pl.BlockSpec((1, d), lambda i: (0, 0)),
        ],
        out_shape=[
            jax.ShapeDtypeStruct((2 * l, d), F32),
            jax.ShapeDtypeStruct((1, d), F32),
        ],
        compiler_params=_cparams(("arbitrary",)),
        name="hyena_filter_taps",
    )(consts["zk"], w1p, vec(b1), w2, vec(b2), w3, vec(b3), vec(freq), w_out,
      decay[:, None, :])

    dtile = min(dtile, d)
    spec_kern = functools.partial(_filter_spec_kernel, cb=cb, nb=nb)
    return pl.pallas_call(
        spec_kern,
        grid=(2 * nb - 1, d // dtile),
        in_specs=[
            _resident((2 * cb, cb), lambda dp, j: (0, 0)),
            pl.BlockSpec((cb, dtile), lambda dp, j: (dp, j)),
            pl.BlockSpec((cb, dtile), lambda dp, j: (dp + 1, j)),
            pl.BlockSpec((1, dtile), lambda dp, j: (0, j)),
            pl.BlockSpec((1, dtile), lambda dp, j: (0, j)),
        ],
        out_specs=pl.BlockSpec((1, 2 * cb, dtile), lambda dp, j: (dp, 0, j)),
        out_shape=jax.ShapeDtypeStruct((2 * nb - 1, 2 * cb, d), F32),
        compiler_params=_cparams(("arbitrary", "arbitrary")),
        name="hyena_filter_spectra",
    )(consts["f_fwd"], kap, kap, ksum, vec(d_bias))


def _hyena_consts(l, cb):
    pos_dim = 33
    bands = (pos_dim - 1) // 2
    t = np.linspace(0.0, 1.0, l)[:, None]
    f = np.linspace(1e-4, bands - 1, bands)
    w = 2.0 * math.pi * np.arange(l)[:, None] / l
    z = np.concatenate([t, np.cos(f * w), -np.sin(f * w)], axis=-1)
    r = np.arange(2 * l)
    valid = r > 0
    pos = np.where(valid, np.abs(r - l), 0)
    zk = np.zeros((2 * l, POS_PAD))
    zk[:, :pos_dim] = z[pos]
    zk[:, DECAY_POS_COL] = t[pos, 0]
    zk[:, MASK_POS_COL] = valid
    k = np.arange(cb)[:, None]
    n = np.arange(cb)[None, :]
    theta = (((2 * k + 1) * n) % (4 * cb)) * (math.pi / (2 * cb))
    cos, sin = np.cos(theta), np.sin(theta)
    f_fwd = np.concatenate([cos, -sin], axis=0)
    f_inv = np.concatenate([cos.T, -sin.T], axis=1) * (1.0 / cb)
    as_f32 = lambda a: jnp.asarray(a, dtype=F32)
    return {"zk": as_f32(zk), "f_fwd": as_f32(f_fwd).astype(BF16), "f_inv": as_f32(f_inv).astype(BF16)}


def _hyena_in_kernel(xp_ref, x_ref, xn_ref, g_ref, w_ref, b_ref, cw_ref, cb_ref, vg_ref, x0_ref,
                     *, dh, dc):
    in_seq_prev = jnp.where(pl.program_id(1) == 0, 0.0, 1.0)
    in_seq_next = jnp.where(pl.program_id(1) == pl.num_programs(1) - 1, 0.0, 1.0)
    tm = x_ref.shape[1]
    nrows = tm + 2 * SUBLANES
    rows = jnp.concatenate([xp_ref[0] * in_seq_prev, x_ref[0], xn_ref[0] * in_seq_next], axis=0)
    h = _rms(rows, g_ref[...]).astype(BF16)
    ridx = lax.broadcasted_iota(jnp.int32, (SUBLANES, dc), 0)
    first_row = jnp.where(ridx == 0, 1.0 - in_seq_prev, 0.0)
    last_row = jnp.where(ridx == SUBLANES - 1, 1.0 - in_seq_next, 0.0)

    def conv_part(col):
        cols = slice(col, col + dc)
        hw = _dot(h, w_ref[:, cols])
        b = b_ref[:, cols]
        cw = cw_ref[:, cols]
        y = (pltpu.roll(hw, 1, axis=0) * cw[0:1] + hw * cw[1:2]
             + pltpu.roll(hw, nrows - 1, axis=0) * cw[2:3])[SUBLANES:tm + SUBLANES]
        y = y + (b * (cw[0:1] + cw[1:2] + cw[2:3]) + cb_ref[:, cols])
        head = y[:SUBLANES] - first_row * (b * cw[0:1])
        tail = y[tm - SUBLANES:] - last_row * (b * cw[2:3])
        return jnp.concatenate([head, y[SUBLANES:tm - SUBLANES], tail], axis=0)

    for c in range(dh // dc):
        x0 = conv_part(c * dc)
        x1 = conv_part(dh + c * dc)
        v = conv_part(2 * dh + c * dc)
        x0_ref[0, :, c * dc:(c + 1) * dc] = x0
        vg_ref[0, :, c * dc:(c + 1) * dc] = (v * x1).astype(BF16)


def _hyena_conv_kernel(v_ref, x0_ref, c_ref, ff_ref, fi_ref, z_ref, vh_ref, yh_ref, *, cb, nb, rc):
    for j in range(nb):
        vh_ref[j] = _dot(ff_ref[...], v_ref[0, j * cb:(j + 1) * cb, :])
    for i in range(nb):
        for r0 in range(0, cb, rc):
            yr = None
            for j in range(nb):
                dlt = i - j + nb - 1
                cr = c_ref[dlt, r0:r0 + rc, :]
                ci = c_ref[dlt, cb + r0:cb + r0 + rc, :]
                vr = vh_ref[j, r0:r0 + rc, :]
                vi = vh_ref[j, cb + r0:cb + r0 + rc, :]
                pr = cr * vr - ci * vi
                pi = cr * vi + ci * vr
                yr, yi = (pr, pi) if yr is None else (yr + pr, yi + pi)
            yh_ref[i, r0:r0 + rc, :] = yr.astype(BF16)
            yh_ref[i, cb + r0:cb + r0 + rc, :] = yi.astype(BF16)
        y = _dot(fi_ref[...], yh_ref[i])
        z_ref[0, i * cb:(i + 1) * cb, :] = (y * x0_ref[0, i * cb:(i + 1) * cb, :]).astype(BF16)


def _hyena_layer(x, g, w_in, b_in, conv_w, conv_b, spectra, consts, cb, tm=512, dc=256, dt=256,
                 rc=16):
    b, l, d = x.shape
    dh = conv_w.shape[1] // 3
    tm = min(tm, l)
    nb = l // cb
    hb = tm // SUBLANES
    nrow8 = l // SUBLANES
    in_kern = functools.partial(_hyena_in_kernel, dh=dh, dc=dc)
    const = lambda shape: _resident(shape, lambda bi, i: (0,) * len(shape))
    vg, x0 = pl.pallas_call(
        in_kern,
        grid=(b, l // tm),
        in_specs=[
            pl.BlockSpec((1, SUBLANES, d), lambda bi, i: (bi, jnp.maximum(i * hb - 1, 0), 0)),
            pl.BlockSpec((1, tm, d), lambda bi, i: (bi, i, 0)),
            pl.BlockSpec((1, SUBLANES, d), lambda bi, i: (bi, jnp.minimum((i + 1) * hb, nrow8 - 1), 0)),
            const((1, d)),
            const((d, 3 * dh)),
            const((1, 3 * dh)),
            const((SHORT_CONV, 3 * dh)),
            const((1, 3 * dh)),
        ],
        out_specs=[
            pl.BlockSpec((1, tm, dh), lambda bi, i: (bi, i, 0)),
            pl.BlockSpec((1, tm, dh), lambda bi, i: (bi, i, 0)),
        ],
        out_shape=[
            jax.ShapeDtypeStruct((b, l, dh), BF16),
            jax.ShapeDtypeStruct((b, l, dh), F32),
        ],
        compiler_params=_cparams(("arbitrary", "arbitrary")),
        name="hyena_in",
    )(x, x, x, g, w_in.astype(BF16), b_in[None, :], conv_w, conv_b[None, :])

    conv_kern = functools.partial(_hyena_conv_kernel, cb=cb, nb=nb, rc=rc)
    z = pl.pallas_call(
        conv_kern,
        grid=(dh // dt, b),
        in_specs=[
            pl.BlockSpec((1, l, dt), lambda j, bi: (bi, 0, j)),
            pl.BlockSpec((1, l, dt), lambda j, bi: (bi, 0, j)),
            _resident((2 * nb - 1, 2 * cb, dt), lambda j, bi: (0, 0, j)),
            _resident((2 * cb, cb), lambda j, bi: (0, 0)),
            _resident((cb, 2 * cb), lambda j, bi: (0, 0)),
        ],
        out_specs=pl.BlockSpec((1, l, dt), lambda j, bi: (bi, 0, j)),
        out_shape=jax.ShapeDtypeStruct((b, l, dh), BF16),
        scratch_shapes=[pltpu.VMEM((nb, 2 * cb, dt), F32), pltpu.VMEM((nb, 2 * cb, dt), BF16)],
        compiler_params=_cparams(("arbitrary", "arbitrary")),
        name="hyena_conv",
    )(vg, x0, spectra, consts["f_fwd"], consts["f_inv"])
    return z


def kernel(x, ffn_norm_g, mix_norm_g, ffn_w_in, ffn_w_out, attn_w_in, attn_q_gain, attn_k_gain,
           attn_w_out, hy_w_in, hy_b_in, hy_conv_w, hy_conv_b, hy_f_w1, hy_f_b1, hy_f_w2, hy_f_b2,
           hy_f_w3, hy_f_b3, hy_f_freq, hy_f_w_out, hy_decay, hy_d_bias, hy_w_out, hy_b_out,
           final_norm_g):
    b, l, d = x.shape
    depth = mix_norm_g.shape[0]
    head_dim = attn_q_gain.shape[1]
    cb = min(CONV_BLOCK, l)
    attn_consts = _attention_consts(l, head_dim, N_Q_HEADS, N_KV_HEADS)
    hy_consts = _hyena_consts(l, cb)
    ffn_w_in = ffn_w_in.astype(BF16)
    ffn_w_out = ffn_w_out.astype(BF16)
    gf = final_norm_g[None, :]
    tm = min(512, b * l)

    def ffn(xx, i, k, final=False, pre=None):
        y = _ffn(xx.reshape(b * l, d), ffn_norm_g[i, k][None, :], ffn_w_in, ffn_w_out, i, k, gf,
                 final, pre=pre, tm=tm)
        return y.reshape(b, l, d)

    for i in range(depth):
        x = ffn(x, i, 0)
        j = i // 2
        g = mix_norm_g[i][None, :]
        pre = None
        if i % 2 == 0:
            x = _attention_layer(x, g, attn_w_in[j], attn_q_gain[j], attn_k_gain[j], attn_w_out[j],
                                 attn_consts)
        else:
            spectra = _hyena_filter_spectra(hy_consts, hy_f_w1[j], hy_f_b1[j], hy_f_w2[j],
                                            hy_f_b2[j], hy_f_w3[j], hy_f_b3[j], hy_f_freq[j],
                                            hy_f_w_out[j], hy_decay[j], hy_d_bias[j], l, cb)
            z = _hyena_layer(x, g, hy_w_in[j], hy_b_in[j], hy_conv_w[j], hy_conv_b[j], spectra,
                             hy_consts, cb)
            pre = (z.reshape(b * l, -1), hy_w_out[j].astype(BF16), hy_b_out[j][None, :])
        x = ffn(x, i, 1, final=(i == depth - 1), pre=pre)
    return x
```

```python
import functools
import math

import jax
import jax.numpy as jnp
import numpy as np
from jax import lax
from jax.experimental import pallas as pl
from jax.experimental.pallas import tpu as pltpu

F32 = jnp.float32
BF16 = jnp.bfloat16

NORM_EPS = 1e-6
FFN_RES_SCALE = 0.5
ROPE_THETA = 10000.0
GRID_W = 64
N_Q_HEADS = 16
N_KV_HEADS = 4
SHORT_CONV = 3
DECAY_POS_COL = 33
MASK_POS_COL = 34
POS_PAD = 40
CONV_BLOCK = 1024

LANES = 128
SUBLANES = 8
BF16_ROWS = 16
VMEM_LIMIT = 56 * 1024 * 1024

HIGHEST = lax.Precision.HIGHEST


def _cparams(sem):
    return pltpu.CompilerParams(dimension_semantics=sem, vmem_limit_bytes=VMEM_LIMIT)


def _resident(shape, index_map):
    return pl.BlockSpec(shape, index_map, pipeline_mode=pl.Buffered(1))


def _rms(x, g):
    ms = jnp.mean(x * x, axis=-1, keepdims=True)
    return x * lax.rsqrt(ms + NORM_EPS) * g


def _dot(a, b):
    return jnp.dot(a, b, preferred_element_type=F32)


def _ffn_kernel(*refs, d_ff, fc, final, pre_proj):
    if pre_proj:
        z_ref, wo_ref, bo_ref, x_ref, g_ref, win_ref, wout_ref, gf_ref, o_ref, acc_ref = refs
        x = x_ref[...] + _dot(z_ref[...], wo_ref[...]) + bo_ref[...]
    else:
        x_ref, g_ref, win_ref, wout_ref, gf_ref, o_ref, acc_ref = refs
        x = x_ref[...]
    h = _rms(x, g_ref[...]).astype(BF16)
    for c in range(d_ff // fc):
        gate = _dot(h, win_ref[:, c * fc:(c + 1) * fc])
        up = _dot(h, win_ref[:, d_ff + c * fc:d_ff + (c + 1) * fc])
        a = (gate * jax.nn.sigmoid(gate) * up).astype(BF16)
        part = _dot(a, wout_ref[c * fc:(c + 1) * fc, :])
        if c == 0:
            acc_ref[...] = part
        else:
            acc_ref[...] += part
    y = x + FFN_RES_SCALE * acc_ref[...]
    if final:
        y = _rms(y, gf_ref[...])
    o_ref[...] = y


def _ffn(x2, g, w_in_all, w_out_all, layer, k, gf, final, pre=None, tm=512, fc=256):
    t, d = x2.shape
    d_ff = w_out_all.shape[2]
    kern = functools.partial(_ffn_kernel, d_ff=d_ff, fc=fc, final=final, pre_proj=pre is not None)
    row_tile = lambda width: pl.BlockSpec((tm, width), lambda i: (i, 0))
    const = lambda shape: _resident(shape, lambda i: (0,) * len(shape))
    in_specs = [
        row_tile(d),
        const((1, d)),
        _resident((None, None, d, 2 * d_ff), lambda i: (layer, k, 0, 0)),
        _resident((None, None, d_ff, d), lambda i: (layer, k, 0, 0)),
        const((1, d)),
    ]
    args = [x2, g, w_in_all, w_out_all, gf]
    if pre is not None:
        z2, w_o, b_o = pre
        in_specs = [row_tile(z2.shape[1]), const(w_o.shape), const((1, d))] + in_specs
        args = [z2, w_o, b_o] + args
    return pl.pallas_call(
        kern,
        grid=(t // tm,),
        in_specs=in_specs,
        out_specs=row_tile(d),
        out_shape=jax.ShapeDtypeStruct((t, d), F32),
        scratch_shapes=[pltpu.VMEM((tm, d), F32)],
        compiler_params=_cparams(("arbitrary",)),
        name="ffn",
    )(*args)


def _head_norm_rope_t(xt, gain_tab, c_tab, up_tab, dn_tab, head_dim):
    rows, tm = xt.shape
    nh = rows // head_dim
    g = jnp.concatenate([gain_tab] * (tm // LANES), axis=-1)
    tc = g * c_tab
    tup = pltpu.roll(g, head_dim - 1, axis=0) * up_tab
    tdn = pltpu.roll(g, 1, axis=0) * dn_tab
    x3 = xt.reshape(nh, head_dim, tm)
    ss = jnp.sum(x3 * x3, axis=1, keepdims=True)
    n = (x3 * lax.rsqrt(ss * (1.0 / head_dim) + NORM_EPS)).reshape(rows, tm)
    per_head = lambda t: jnp.concatenate([t] * nh, axis=0)
    return (n * per_head(tc) + pltpu.roll(n, rows - 1, axis=0) * per_head(tup)
            + pltpu.roll(n, 1, axis=0) * per_head(tdn))


def _qkv_kernel(x_ref, g_ref, w_ref, qg_ref, kg_ref, cq_ref, upq_ref, dnq_ref, ck_ref, upk_ref,
                dnk_ref, qt_ref, k_ref, vt_ref, *, d_attn, d_kv, head_dim):
    h = _rms(x_ref[0], g_ref[...]).astype(BF16)
    qkv = _dot(h, w_ref[...])
    tm = qkv.shape[0]
    qt = _head_norm_rope_t(qkv[:, :d_attn].T, qg_ref[...], cq_ref[...], upq_ref[...],
                           dnq_ref[...], head_dim)
    kt = _head_norm_rope_t(qkv[:, d_attn:d_attn + d_kv].T, kg_ref[...], ck_ref[...], upk_ref[...],
                           dnk_ref[...], head_dim)
    qt_ref[0] = qt.astype(BF16)
    kr = kt.T
    vt = qkv[:, d_attn + d_kv:].T
    ones_row = jnp.where(lax.broadcasted_iota(jnp.int32, (BF16_ROWS, tm), 0) == 0, 1.0, 0.0)
    for gidx in range(d_kv // head_dim):
        k_ref[0, gidx] = kr[:, gidx * head_dim:(gidx + 1) * head_dim].astype(BF16)
        vt_ref[0, gidx, :head_dim, :] = vt[gidx * head_dim:(gidx + 1) * head_dim].astype(BF16)
        vt_ref[0, gidx, head_dim:, :] = ones_row.astype(BF16)


def _attn_kernel(x_ref, qt_ref, k_ref, vt_ref, wot_ref, o_ref, s0_ref, s1_ref, m0_ref, m1_ref,
                 ot_ref, *, n_heads, q_per_kv, head_dim, key_splits):
    n_keys = k_ref.shape[2]
    kc = n_keys // key_splits

    def scores(hd, s_ref, m_ref):
        grp = hd // q_per_kv
        qt = qt_ref[0, hd]
        m = None
        for c in range(key_splits):
            sv = _dot(k_ref[0, grp, c * kc:(c + 1) * kc, :], qt)
            s_ref[c * kc:(c + 1) * kc, :] = sv
            mc = jnp.max(sv, axis=0, keepdims=True)
            m = mc if m is None else jnp.maximum(m, mc)
        m_ref[...] = m

    def finish(hd, s_ref, m_ref):
        grp = hd // q_per_kv
        m = m_ref[...]
        oa = None
        for c in range(key_splits):
            p = jnp.exp2(s_ref[c * kc:(c + 1) * kc, :] - m).astype(BF16)
            part = _dot(vt_ref[0, grp, :, c * kc:(c + 1) * kc], p)
            oa = part if oa is None else oa + part
        row = pl.multiple_of(hd * head_dim, head_dim)
        ot_ref[pl.ds(row, head_dim), :] = (oa[:head_dim] / oa[head_dim:head_dim + 1]).astype(BF16)

    def head_pair(i, carry):
        h0 = 2 * i
        scores(h0 + 1, s1_ref, m1_ref)
        finish(h0, s0_ref, m0_ref)
        scores(h0 + 2, s0_ref, m0_ref)
        finish(h0 + 1, s1_ref, m1_ref)
        return carry

    scores(0, s0_ref, m0_ref)
    lax.fori_loop(0, n_heads // 2 - 1, head_pair, 0)
    scores(n_heads - 1, s1_ref, m1_ref)
    finish(n_heads - 2, s0_ref, m0_ref)
    finish(n_heads - 1, s1_ref, m1_ref)
    yt = _dot(wot_ref[...], ot_ref[...])
    o_ref[0] = x_ref[0] + yt.T


def _attention_layer(x, g, w_in, q_gain, k_gain, w_out, consts, tm=512, tq=256, key_splits=2):
    b, l, d = x.shape
    head_dim = q_gain.shape[0]
    d_attn = w_out.shape[0]
    d_kv = (w_in.shape[1] - d_attn) // 2
    n_heads = d_attn // head_dim
    n_kv = d_kv // head_dim
    vrows = head_dim + BF16_ROWS
    tm = min(tm, l)
    tq = min(tq, l)
    tab = pl.BlockSpec((head_dim, tm), lambda bi, i: (0, i))
    const = lambda shape: _resident(shape, lambda bi, i: (0,) * len(shape))
    in_lanes = lambda gain: jnp.broadcast_to(gain[:, None], (head_dim, LANES))
    qkv_kern = functools.partial(_qkv_kernel, d_attn=d_attn, d_kv=d_kv, head_dim=head_dim)
    qt, k, vt = pl.pallas_call(
        qkv_kern,
        grid=(b, l // tm),
        in_specs=[
            pl.BlockSpec((1, tm, d), lambda bi, i: (bi, i, 0)),
            const((1, d)),
            const((d, d_attn + 2 * d_kv)),
            const((head_dim, LANES)),
            const((head_dim, LANES)),
            tab, tab, tab, tab, tab, tab,
        ],
        out_specs=[
            pl.BlockSpec((1, d_attn, tm), lambda bi, i: (bi, 0, i)),
            pl.BlockSpec((1, n_kv, tm, head_dim), lambda bi, i: (bi, 0, i, 0)),
            pl.BlockSpec((1, n_kv, vrows, tm), lambda bi, i: (bi, 0, 0, i)),
        ],
        out_shape=[
            jax.ShapeDtypeStruct((b, d_attn, l), BF16),
            jax.ShapeDtypeStruct((b, n_kv, l, head_dim), BF16),
            jax.ShapeDtypeStruct((b, n_kv, vrows, l), BF16),
        ],
        compiler_params=_cparams(("arbitrary", "arbitrary")),
        name="attn_qkv",
    )(x, g, w_in.astype(BF16), in_lanes(q_gain), in_lanes(k_gain),
      consts["cq"], consts["upq"], consts["dnq"], consts["ck"], consts["upk"], consts["dnk"])

    qt = qt.reshape(b, n_heads, head_dim, l)
    attn_kern = functools.partial(_attn_kernel, n_heads=n_heads, q_per_kv=n_heads // n_kv,
                                  head_dim=head_dim, key_splits=key_splits)
    return pl.pallas_call(
        attn_kern,
        grid=(b, l // tq),
        in_specs=[
            pl.BlockSpec((1, tq, d), lambda bi, i: (bi, i, 0)),
            pl.BlockSpec((1, n_heads, head_dim, tq), lambda bi, i: (bi, 0, 0, i)),
            pl.BlockSpec((1, n_kv, l, head_dim), lambda bi, i: (bi, 0, 0, 0)),
            pl.BlockSpec((1, n_kv, vrows, l), lambda bi, i: (bi, 0, 0, 0)),
            _resident((d, d_attn), lambda bi, i: (0, 0)),
        ],
        out_specs=pl.BlockSpec((1, tq, d), lambda bi, i: (bi, i, 0)),
        out_shape=jax.ShapeDtypeStruct((b, l, d), F32),
        scratch_shapes=[pltpu.VMEM((l, tq), F32), pltpu.VMEM((l, tq), F32),
                        pltpu.VMEM((1, tq), F32), pltpu.VMEM((1, tq), F32),
                        pltpu.VMEM((d_attn, tq), BF16)],
        compiler_params=_cparams(("arbitrary", "arbitrary")),
        name="attn_core",
    )(x, qt, k, vt, w_out.T.astype(BF16))


def _attention_consts(l, head_dim, n_heads, n_kv):
    rows = l // GRID_W
    axis_dim = head_dim // 2
    row = jnp.repeat(jnp.arange(rows, dtype=F32), GRID_W)
    col = jnp.tile(jnp.arange(GRID_W, dtype=F32), rows)
    inv = ROPE_THETA ** (-jnp.arange(0, axis_dim, 2, dtype=F32) / axis_dim)
    ang = jnp.concatenate([row[:, None] * inv, col[:, None] * inv], axis=-1)
    cos = jnp.repeat(jnp.cos(ang), 2, axis=-1).T
    sin = jnp.repeat(jnp.sin(ang), 2, axis=-1).T
    odd = (jnp.arange(head_dim) % 2 == 1)[:, None]
    up = jnp.where(odd, 0.0, -sin)
    dn = jnp.where(odd, sin, 0.0)
    q_scale = (head_dim ** -0.5) * math.log2(math.e)
    return {"cq": cos * q_scale, "upq": up * q_scale, "dnq": dn * q_scale,
            "ck": cos, "upk": up, "dnk": dn}


def _filter_taps_kernel(zk_ref, w1_ref, b1_ref, w2_ref, b2_ref, w3_ref, b3_ref, fr_ref, wo_ref,
                        dec_ref, kap_ref, sum_ref):
    zk = zk_ref[...]
    fr = fr_ref[...]
    hid = jnp.sin(fr * (jnp.dot(zk, w1_ref[...], precision=HIGHEST, preferred_element_type=F32)
                        + b1_ref[...]))
    hid = jnp.sin(fr * (jnp.dot(hid, w2_ref[...], precision=HIGHEST, preferred_element_type=F32)
                        + b2_ref[...]))
    hid = jnp.sin(fr * (jnp.dot(hid, w3_ref[...], precision=HIGHEST, preferred_element_type=F32)
                        + b3_ref[...]))
    filt = jnp.dot(hid, wo_ref[...], precision=HIGHEST, preferred_element_type=F32)
    t = zk[:, DECAY_POS_COL:DECAY_POS_COL + 1]
    mask = zk[:, MASK_POS_COL:MASK_POS_COL + 1]
    kap = filt * jnp.exp(-t * jnp.abs(dec_ref[0])) * mask
    kap_ref[...] = kap

    @pl.when(pl.program_id(0) == 0)
    def _():
        sum_ref[...] = jnp.zeros_like(sum_ref)

    sum_ref[...] += jnp.sum(jnp.abs(kap), axis=0, keepdims=True)


def _filter_spec_kernel(f_ref, klo_ref, khi_ref, sum_ref, dbias_ref, c_ref, *, cb, nb):
    dp = pl.program_id(0)
    inv_sum = 1.0 / sum_ref[...]
    dbias = dbias_ref[...]

    def spectrum(k_ref, is_zero_lag_block):
        kap = k_ref[...]
        hi = kap.astype(BF16)
        lo = (kap - hi.astype(F32)).astype(BF16)
        a = (_dot(f_ref[...], hi) + _dot(f_ref[...], lo)) * inv_sum
        return a[:cb] + jnp.where(is_zero_lag_block, dbias, 0.0), a[cb:]

    ar1, ai1 = spectrum(khi_ref, dp + 1 == nb)
    ar0, ai0 = spectrum(klo_ref, dp == nb)
    kidx = lax.broadcasted_iota(jnp.int32, ar1.shape, 0)
    sgn = (1 - 2 * (kidx % 2)).astype(F32)
    c_ref[0, :cb, :] = ar1 - sgn * ai0
    c_ref[0, cb:, :] = ai1 + sgn * ar0


def _hyena_filter_spectra(consts, w1, b1, w2, b2, w3, b3, freq, w_out, decay, d_bias, l, cb,
                          tr=512, dtile=512):
    d = d_bias.shape[0]
    nb = l // cb
    tr = min(tr, l)
    hidden = w2.shape[0]
    w1p = jnp.zeros((POS_PAD, hidden), F32).at[:w1.shape[0]].set(w1)
    half_steps = l // tr
    vec = lambda a: a[None, :]
    const = lambda shape: _resident(shape, lambda i: (0,) * len(shape))
    kap, ksum = pl.pallas_call(
        _filter_taps_kernel,
        grid=(2 * l // tr,),
        in_specs=[
            pl.BlockSpec((tr, POS_PAD), lambda i: (i, 0)),
            const((POS_PAD, hidden)), const((1, hidden)),
            const((hidden, hidden)), const((1, hidden)),
            const((hidden, hidden)), const((1, hidden)),
            const((1, hidden)),
            pl.BlockSpec((hidden, d), lambda i: (0, jnp.where(i < half_steps, 1, 0))),
            pl.BlockSpec((1, 1, d), lambda i: (jnp.where(i < half_steps, 1, 0), 0, 0)),
        ],
        out_specs=[
            pl.BlockSpec((tr, d), lambda i: (i, 0)),
            pl.BlockSpec((1, d), lambda i: (0, 0)),
        ],
        out_shape=[
            jax.ShapeDtypeStruct((2 * l, d), F32),
            jax.ShapeDtypeStruct((1, d), F32),
        ],
        compiler_params=_cparams(("arbitrary",)),
        name="hyena_filter_taps",
    )(consts["zk"], w1p, vec(b1), w2, vec(b2), w3, vec(b3), vec(freq), w_out,
      decay[:, None, :])

    dtile = min(dtile, d)
    spec_kern = functools.partial(_filter_spec_kernel, cb=cb, nb=nb)
    return pl.pallas_call(
        spec_kern,
        grid=(2 * nb - 1, d // dtile),
        in_specs=[
            _resident((2 * cb, cb), lambda dp, j: (0, 0)),
            pl.BlockSpec((cb, dtile), lambda dp, j: (dp, j)),
            pl.BlockSpec((cb, dtile), lambda dp, j: (dp + 1, j)),
            pl.BlockSpec((1, dtile), lambda dp, j: (0, j)),
            pl.BlockSpec((1, dtile), lambda dp, j: (0, j)),
        ],
        out_specs=pl.BlockSpec((1, 2 * cb, dtile), lambda dp, j: (dp, 0, j)),
        out_shape=jax.ShapeDtypeStruct((2 * nb - 1, 2 * cb, d), F32),
        compiler_params=_cparams(("arbitrary", "arbitrary")),
        name="hyena_filter_spectra",
    )(consts["f_fwd"], kap, kap, ksum, vec(d_bias))


def _hyena_consts(l, cb):
    pos_dim = 33
    bands = (pos_dim - 1) // 2
    t = np.linspace(0.0, 1.0, l)[:, None]
    f = np.linspace(1e-4, bands - 1, bands)
    w = 2.0 * math.pi * np.arange(l)[:, None] / l
    z = np.concatenate([t, np.cos(f * w), -np.sin(f * w)], axis=-1)
    r = np.arange(2 * l)
    valid = r > 0
    pos = np.where(valid, np.abs(r - l), 0)
    zk = np.zeros((2 * l, POS_PAD))
    zk[:, :pos_dim] = z[pos]
    zk[:, DECAY_POS_COL] = t[pos, 0]
    zk[:, MASK_POS_COL] = valid
    k = np.arange(cb)[:, None]
    n = np.arange(cb)[None, :]
    theta = (((2 * k + 1) * n) % (4 * cb)) * (math.pi / (2 * cb))
    cos, sin = np.cos(theta), np.sin(theta)
    f_fwd = np.concatenate([cos, -sin], axis=0)
    f_inv = np.concatenate([cos.T, -sin.T], axis=1) * (1.0 / cb)
    as_f32 = lambda a: jnp.asarray(a, dtype=F32)
    return {"zk": as_f32(zk), "f_fwd": as_f32(f_fwd).astype(BF16), "f_inv": as_f32(f_inv).astype(BF16)}


def _hyena_in_kernel(xp_ref, x_ref, xn_ref, g_ref, w_ref, b_ref, cw_ref, cb_ref, vg_ref, x0_ref,
                     *, dh, dc):
    in_seq_prev = jnp.where(pl.program_id(1) == 0, 0.0, 1.0)
    in_seq_next = jnp.where(pl.program_id(1) == pl.num_programs(1) - 1, 0.0, 1.0)
    tm = x_ref.shape[1]
    nrows = tm + 2 * SUBLANES
    rows = jnp.concatenate([xp_ref[0] * in_seq_prev, x_ref[0], xn_ref[0] * in_seq_next], axis=0)
    h = _rms(rows, g_ref[...]).astype(BF16)
    ridx = lax.broadcasted_iota(jnp.int32, (SUBLANES, dc), 0)
    first_row = jnp.where(ridx == 0, 1.0 - in_seq_prev, 0.0)
    last_row = jnp.where(ridx == SUBLANES - 1, 1.0 - in_seq_next, 0.0)

    def conv_part(col):
        cols = slice(col, col + dc)
        hw = _dot(h, w_ref[:, cols])
        b = b_ref[:, cols]
        cw = cw_ref[:, cols]
        y = (pltpu.roll(hw, 1, axis=0) * cw[0:1] + hw * cw[1:2]
             + pltpu.roll(hw, nrows - 1, axis=0) * cw[2:3])[SUBLANES:tm + SUBLANES]
        y = y + (b * (cw[0:1] + cw[1:2] + cw[2:3]) + cb_ref[:, cols])
        head = y[:SUBLANES] - first_row * (b * cw[0:1])
        tail = y[tm - SUBLANES:] - last_row * (b * cw[2:3])
        return jnp.concatenate([head, y[SUBLANES:tm - SUBLANES], tail], axis=0)

    for c in range(dh // dc):
        x0 = conv_part(c * dc)
        x1 = conv_part(dh + c * dc)
        v = conv_part(2 * dh + c * dc)
        x0_ref[0, :, c * dc:(c + 1) * dc] = x0
        vg_ref[0, :, c * dc:(c + 1) * dc] = (v * x1).astype(BF16)


def _hyena_conv_kernel(v_ref, x0_ref, c_ref, ff_ref, fi_ref, z_ref, vh_ref, yh_ref, *, cb, nb, rc):
    for j in range(nb):
        vh_ref[j] = _dot(ff_ref[...], v_ref[0, j * cb:(j + 1) * cb, :])
    for i in range(nb):
        for r0 in range(0, cb, rc):
            yr = None
            for j in range(nb):
                dlt = i - j + nb - 1
                cr = c_ref[dlt, r0:r0 + rc, :]
                ci = c_ref[dlt, cb + r0:cb + r0 + rc, :]
                vr = vh_ref[j, r0:r0 + rc, :]
                vi = vh_ref[j, cb + r0:cb + r0 + rc, :]
                pr = cr * vr - ci * vi
                pi = cr * vi + ci * vr
                yr, yi = (pr, pi) if yr is None else (yr + pr, yi + pi)
            yh_ref[i, r0:r0 + rc, :] = yr.astype(BF16)
            yh_ref[i, cb + r0:cb + r0 + rc, :] = yi.astype(BF16)
        y = _dot(fi_ref[...], yh_ref[i])
        z_ref[0, i * cb:(i + 1) * cb, :] = (y * x0_ref[0, i * cb:(i + 1) * cb, :]).astype(BF16)


def _hyena_layer(x, g, w_in, b_in, conv_w, conv_b, spectra, consts, cb, tm=512, dc=256, dt=256,
                 rc=16):
    b, l, d = x.shape
    dh = conv_w.shape[1] // 3
    tm = min(tm, l)
    nb = l // cb
    hb = tm // SUBLANES
    nrow8 = l // SUBLANES
    in_kern = functools.partial(_hyena_in_kernel, dh=dh, dc=dc)
    const = lambda shape: _resident(shape, lambda bi, i: (0,) * len(shape))
    vg, x0 = pl.pallas_call(
        in_kern,
        grid=(b, l // tm),
        in_specs=[
            pl.BlockSpec((1, SUBLANES, d), lambda bi, i: (bi, jnp.maximum(i * hb - 1, 0), 0)),
            pl.BlockSpec((1, tm, d), lambda bi, i: (bi, i, 0)),
            pl.BlockSpec((1, SUBLANES, d), lambda bi, i: (bi, jnp.minimum((i + 1) * hb, nrow8 - 1), 0)),
            const((1, d)),
            const((d, 3 * dh)),
            const((1, 3 * dh)),
            const((SHORT_CONV, 3 * dh)),
            const((1, 3 * dh)),
        ],
        out_specs=[
            pl.BlockSpec((1, tm, dh), lambda bi, i: (bi, i, 0)),
            pl.BlockSpec((1, tm, dh), lambda bi, i: (bi, i, 0)),
        ],
        out_shape=[
            jax.ShapeDtypeStruct((b, l, dh), BF16),
            jax.ShapeDtypeStruct((b, l, dh), F32),
        ],
        compiler_params=_cparams(("arbitrary", "arbitrary")),
        name="hyena_in",
    )(x, x, x, g, w_in.astype(BF16), b_in[None, :], conv_w, conv_b[None, :])

    conv_kern = functools.partial(_hyena_conv_kernel, cb=cb, nb=nb, rc=rc)
    z = pl.pallas_call(
        conv_kern,
        grid=(dh // dt, b),
        in_specs=[
            pl.BlockSpec((1, l, dt), lambda j, bi: (bi, 0, j)),
            pl.BlockSpec((1, l, dt), lambda j, bi: (bi, 0, j)),
            _resident((2 * nb - 1, 2 * cb, dt), lambda j, bi: (0, 0, j)),
            _resident((2 * cb, cb), lambda j, bi: (0, 0)),
            _resident((cb, 2 * cb), lambda j, bi: (0, 0)),
        ],
        out_specs=pl.BlockSpec((1, l, dt), lambda j, bi: (bi, 0, j)),
        out_shape=jax.ShapeDtypeStruct((b, l, dh), BF16),
        scratch_shapes=[pltpu.VMEM((nb, 2 * cb, dt), F32), pltpu.VMEM((nb, 2 * cb, dt), BF16)],
        compiler_params=_cparams(("arbitrary", "arbitrary")),
        name="hyena_conv",
    )(vg, x0, spectra, consts["f_fwd"], consts["f_inv"])
    return z


def kernel(x, ffn_norm_g, mix_norm_g, ffn_w_in, ffn_w_out, attn_w_in, attn_q_gain, attn_k_gain,
           attn_w_out, hy_w_in, hy_b_in, hy_conv_w, hy_conv_b, hy_f_w1, hy_f_b1, hy_f_w2, hy_f_b2,
           hy_f_w3, hy_f_b3, hy_f_freq, hy_f_w_out, hy_decay, hy_d_bias, hy_w_out, hy_b_out,
           final_norm_g):
    b, l, d = x.shape
    depth = mix_norm_g.shape[0]
    head_dim = attn_q_gain.shape[1]
    cb = min(CONV_BLOCK, l)
    attn_consts = _attention_consts(l, head_dim, N_Q_HEADS, N_KV_HEADS)
    hy_consts = _hyena_consts(l, cb)
    ffn_w_in = ffn_w_in.astype(BF16)
    ffn_w_out = ffn_w_out.astype(BF16)
    gf = final_norm_g[None, :]
    tm = min(512, b * l)

    def ffn(xx, i, k, final=False, pre=None):
        y = _ffn(xx.reshape(b * l, d), ffn_norm_g[i, k][None, :], ffn_w_in, ffn_w_out, i, k, gf,
                 final, pre=pre, tm=tm)
        return y.reshape(b, l, d)

    for i in range(depth):
        x = ffn(x, i, 0)
        j = i // 2
        g = mix_norm_g[i][None, :]
        pre = None
        if i % 2 == 0:
            x = _attention_layer(x, g, attn_w_in[j], attn_q_gain[j], attn_k_gain[j], attn_w_out[j],
                                 attn_consts)
        else:
            spectra = _hyena_filter_spectra(hy_consts, hy_f_w1[j], hy_f_b1[j], hy_f_w2[j],
                                            hy_f_b2[j], hy_f_w3[j], hy_f_b3[j], hy_f_freq[j],
                                            hy_f_w_out[j], hy_decay[j], hy_d_bias[j], l, cb)
            z = _hyena_layer(x, g, hy_w_in[j], hy_b_in[j], hy_conv_w[j], hy_conv_b[j], spectra,
                             hy_consts, cb)
            pre = (z.reshape(b * l, -1), hy_w_out[j].astype(BF16), hy_b_out[j][None, :])
        x = ffn(x, i, 1, final=(i == depth - 1), pre=pre)
    return x
```

```python
import functools
import math

import jax
import jax.numpy as jnp
import numpy as np
from jax import lax
from jax.experimental import pallas as pl
from jax.experimental.pallas import tpu as pltpu

F32 = jnp.float32
BF16 = jnp.bfloat16

NORM_EPS = 1e-6
FFN_RES_SCALE = 0.5
ROPE_THETA = 10000.0
GRID_W = 64
N_Q_HEADS = 16
N_KV_HEADS = 4
SHORT_CONV = 3
DECAY_POS_COL = 33
MASK_POS_COL = 34
POS_PAD = 40
CONV_BLOCK = 1024

LANES = 128
SUBLANES = 8
BF16_ROWS = 16
VMEM_LIMIT = 56 * 1024 * 1024

HIGHEST = lax.Precision.HIGHEST


def _cparams(sem):
    return pltpu.CompilerParams(dimension_semantics=sem, vmem_limit_bytes=VMEM_LIMIT)


def _resident(shape, index_map):
    return pl.BlockSpec(shape, index_map, pipeline_mode=pl.Buffered(1))


def _rms(x, g):
    ms = jnp.mean(x * x, axis=-1, keepdims=True)
    return x * lax.rsqrt(ms + NORM_EPS) * g


def _dot(a, b):
    return jnp.dot(a, b, preferred_element_type=F32)


def _ffn_kernel(*refs, d_ff, fc, final, pre_proj):
    if pre_proj:
        z_ref, wo_ref, bo_ref, x_ref, g_ref, win_ref, wout_ref, gf_ref, o_ref, acc_ref = refs
        x = x_ref[...] + _dot(z_ref[...], wo_ref[...]) + bo_ref[...]
    else:
        x_ref, g_ref, win_ref, wout_ref, gf_ref, o_ref, acc_ref = refs
        x = x_ref[...]
    h = _rms(x, g_ref[...]).astype(BF16)
    for c in range(d_ff // fc):
        gate = _dot(h, win_ref[:, c * fc:(c + 1) * fc])
        up = _dot(h, win_ref[:, d_ff + c * fc:d_ff + (c + 1) * fc])
        a = (gate * jax.nn.sigmoid(gate) * up).astype(BF16)
        part = _dot(a, wout_ref[c * fc:(c + 1) * fc, :])
        if c == 0:
            acc_ref[...] = part
        else:
            acc_ref[...] += part
    y = x + FFN_RES_SCALE * acc_ref[...]
    if final:
        y = _rms(y, gf_ref[...])
    o_ref[...] = y


def _ffn(x2, g, w_in_all, w_out_all, layer, k, gf, final, pre=None, tm=512, fc=256):
    t, d = x2.shape
    d_ff = w_out_all.shape[2]
    kern = functools.partial(_ffn_kernel, d_ff=d_ff, fc=fc, final=final, pre_proj=pre is not None)
    row_tile = lambda width: pl.BlockSpec((tm, width), lambda i: (i, 0))
    const = lambda shape: _resident(shape, lambda i: (0,) * len(shape))
    in_specs = [
        row_tile(d),
        const((1, d)),
        _resident((None, None, d, 2 * d_ff), lambda i: (layer, k, 0, 0)),
        _resident((None, None, d_ff, d), lambda i: (layer, k, 0, 0)),
        const((1, d)),
    ]
    args = [x2, g, w_in_all, w_out_all, gf]
    if pre is not None:
        z2, w_o, b_o = pre
        in_specs = [row_tile(z2.shape[1]), const(w_o.shape), const((1, d))] + in_specs
        args = [z2, w_o, b_o] + args
    return pl.pallas_call(
        kern,
        grid=(t // tm,),
        in_specs=in_specs,
        out_specs=row_tile(d),
        out_shape=jax.ShapeDtypeStruct((t, d), F32),
        scratch_shapes=[pltpu.VMEM((tm, d), F32)],
        compiler_params=_cparams(("arbitrary",)),
        name="ffn",
    )(*args)


def _head_norm_rope_t(xt, gain_tab, c_tab, up_tab, dn_tab, head_dim):
    rows, tm = xt.shape
    nh = rows // head_dim
    g = jnp.concatenate([gain_tab] * (tm // LANES), axis=-1)
    tc = g * c_tab
    tup = pltpu.roll(g, head_dim - 1, axis=0) * up_tab
    tdn = pltpu.roll(g, 1, axis=0) * dn_tab
    x3 = xt.reshape(nh, head_dim, tm)
    ss = jnp.sum(x3 * x3, axis=1, keepdims=True)
    n = (x3 * lax.rsqrt(ss * (1.0 / head_dim) + NORM_EPS)).reshape(rows, tm)
    per_head = lambda t: jnp.concatenate([t] * nh, axis=0)
    return (n * per_head(tc) + pltpu.roll(n, rows - 1, axis=0) * per_head(tup)
            + pltpu.roll(n, 1, axis=0) * per_head(tdn))


def _qkv_kernel(x_ref, g_ref, w_ref, qg_ref, kg_ref, cq_ref, upq_ref, dnq_ref, ck_ref, upk_ref,
                dnk_ref, qt_ref, k_ref, vt_ref, *, d_attn, d_kv, head_dim):
    h = _rms(x_ref[0], g_ref[...]).astype(BF16)
    qkv = _dot(h, w_ref[...])
    tm = qkv.shape[0]
    qt = _head_norm_rope_t(qkv[:, :d_attn].T, qg_ref[...], cq_ref[...], upq_ref[...],
                           dnq_ref[...], head_dim)
    kt = _head_norm_rope_t(qkv[:, d_attn:d_attn + d_kv].T, kg_ref[...], ck_ref[...], upk_ref[...],
                           dnk_ref[...], head_dim)
    qt_ref[0] = qt.astype(BF16)
    kr = kt.T
    vt = qkv[:, d_attn + d_kv:].T
    ones_row = jnp.where(lax.broadcasted_iota(jnp.int32, (BF16_ROWS, tm), 0) == 0, 1.0, 0.0)
    for gidx in range(d_kv // head_dim):
        k_ref[0, gidx] = kr[:, gidx * head_dim:(gidx + 1) * head_dim].astype(BF16)
        vt_ref[0, gidx, :head_dim, :] = vt[gidx * head_dim:(gidx + 1) * head_dim].astype(BF16)
        vt_ref[0, gidx, head_dim:, :] = ones_row.astype(BF16)


def _attn_kernel(x_ref, qt_ref, k_ref, vt_ref, wot_ref, o_ref, s0_ref, s1_ref, m0_ref, m1_ref,
                 ot_ref, *, n_heads, q_per_kv, head_dim, key_splits):
    n_keys = k_ref.shape[2]
    kc = n_keys // key_splits

    def scores(hd, s_ref, m_ref):
        grp = hd // q_per_kv
        qt = qt_ref[0, hd]
        m = None
        for c in range(key_splits):
            sv = _dot(k_ref[0, grp, c * kc:(c + 1) * kc, :], qt)
            s_ref[c * kc:(c + 1) * kc, :] = sv
            mc = jnp.max(sv, axis=0, keepdims=True)
            m = mc if m is None else jnp.maximum(m, mc)
        m_ref[...] = m

    def finish(hd, s_ref, m_ref):
        grp = hd // q_per_kv
        m = m_ref[...]
        p = jnp.exp2(s_ref[...] - m).astype(BF16)
        oa = _dot(vt_ref[0, grp], p)
        row = pl.multiple_of(hd * head_dim, head_dim)
        ot_ref[pl.ds(row, head_dim), :] = (oa[:head_dim] / oa[head_dim:head_dim + 1]).astype(BF16)

    def head_pair(i, carry):
        h0 = 2 * i
        scores(h0 + 1, s1_ref, m1_ref)
        finish(h0, s0_ref, m0_ref)
        scores(h0 + 2, s0_ref, m0_ref)
        finish(h0 + 1, s1_ref, m1_ref)
        return carry

    scores(0, s0_ref, m0_ref)
    lax.fori_loop(0, n_heads // 2 - 1, head_pair, 0)
    scores(n_heads - 1, s1_ref, m1_ref)
    finish(n_heads - 2, s0_ref, m0_ref)
    finish(n_heads - 1, s1_ref, m1_ref)
    yt = _dot(wot_ref[...], ot_ref[...])
    o_ref[0] = x_ref[0] + yt.T


def _attention_layer(x, g, w_in, q_gain, k_gain, w_out, consts, tm=512, tq=256, key_splits=2):
    b, l, d = x.shape
    head_dim = q_gain.shape[0]
    d_attn = w_out.shape[0]
    d_kv = (w_in.shape[1] - d_attn) // 2
    n_heads = d_attn // head_dim
    n_kv = d_kv // head_dim
    vrows = head_dim + BF16_ROWS
    tm = min(tm, l)
    tq = min(tq, l)
    tab = pl.BlockSpec((head_dim, tm), lambda bi, i: (0, i))
    const = lambda shape: _resident(shape, lambda bi, i: (0,) * len(shape))
    in_lanes = lambda gain: jnp.broadcast_to(gain[:, None], (head_dim, LANES))
    qkv_kern = functools.partial(_qkv_kernel, d_attn=d_attn, d_kv=d_kv, head_dim=head_dim)
    qt, k, vt = pl.pallas_call(
        qkv_kern,
        grid=(b, l // tm),
        in_specs=[
            pl.BlockSpec((1, tm, d), lambda bi, i: (bi, i, 0)),
            const((1, d)),
            const((d, d_attn + 2 * d_kv)),
            const((head_dim, LANES)),
            const((head_dim, LANES)),
            tab, tab, tab, tab, tab, tab,
        ],
        out_specs=[
            pl.BlockSpec((1, d_attn, tm), lambda bi, i: (bi, 0, i)),
            pl.BlockSpec((1, n_kv, tm, head_dim), lambda bi, i: (bi, 0, i, 0)),
            pl.BlockSpec((1, n_kv, vrows, tm), lambda bi, i: (bi, 0, 0, i)),
        ],
        out_shape=[
            jax.ShapeDtypeStruct((b, d_attn, l), BF16),
            jax.ShapeDtypeStruct((b, n_kv, l, head_dim), BF16),
            jax.ShapeDtypeStruct((b, n_kv, vrows, l), BF16),
        ],
        compiler_params=_cparams(("arbitrary", "arbitrary")),
        name="attn_qkv",
    )(x, g, w_in.astype(BF16), in_lanes(q_gain), in_lanes(k_gain),
      consts["cq"], consts["upq"], consts["dnq"], consts["ck"], consts["upk"], consts["dnk"])

    qt = qt.reshape(b, n_heads, head_dim, l)
    attn_kern = functools.partial(_attn_kernel, n_heads=n_heads, q_per_kv=n_heads // n_kv,
                                  head_dim=head_dim, key_splits=key_splits)
    return pl.pallas_call(
        attn_kern,
        grid=(b, l // tq),
        in_specs=[
            pl.BlockSpec((1, tq, d), lambda bi, i: (bi, i, 0)),
            pl.BlockSpec((1, n_heads, head_dim, tq), lambda bi, i: (bi, 0, 0, i)),
            pl.BlockSpec((1, n_kv, l, head_dim), lambda bi, i: (bi, 0, 0, 0)),
            pl.BlockSpec((1, n_kv, vrows, l), lambda bi, i: (bi, 0, 0, 0)),
            _resident((d, d_attn), lambda bi, i: (0, 0)),
        ],
        out_specs=pl.BlockSpec((1, tq, d), lambda bi, i: (bi, i, 0)),
        out_shape=jax.ShapeDtypeStruct((b, l, d), F32),
        scratch_shapes=[pltpu.VMEM((l, tq), F32), pltpu.VMEM((l, tq), F32),
                        pltpu.VMEM((1, tq), F32), pltpu.VMEM((1, tq), F32),
                        pltpu.VMEM((d_attn, tq), BF16)],
        compiler_params=_cparams(("arbitrary", "arbitrary")),
        name="attn_core",
    )(x, qt, k, vt, w_out.T.astype(BF16))


def _attention_consts(l, head_dim, n_heads, n_kv):
    rows = l // GRID_W
    axis_dim = head_dim // 2
    row = jnp.repeat(jnp.arange(rows, dtype=F32), GRID_W)
    col = jnp.tile(jnp.arange(GRID_W, dtype=F32), rows)
    inv = ROPE_THETA ** (-jnp.arange(0, axis_dim, 2, dtype=F32) / axis_dim)
    ang = jnp.concatenate([row[:, None] * inv, col[:, None] * inv], axis=-1)
    cos = jnp.repeat(jnp.cos(ang), 2, axis=-1).T
    sin = jnp.repeat(jnp.sin(ang), 2, axis=-1).T
    odd = (jnp.arange(head_dim) % 2 == 1)[:, None]
    up = jnp.where(odd, 0.0, -sin)
    dn = jnp.where(odd, sin, 0.0)
    q_scale = (head_dim ** -0.5) * math.log2(math.e)
    return {"cq": cos * q_scale, "upq": up * q_scale, "dnq": dn * q_scale,
            "ck": cos, "upk": up, "dnk": dn}


def _filter_taps_kernel(zk_ref, w1_ref, b1_ref, w2_ref, b2_ref, w3_ref, b3_ref, fr_ref, wo_ref,
                        dec_ref, kap_ref, sum_ref):
    zk = zk_ref[...]
    fr = fr_ref[...]
    hid = jnp.sin(fr * (jnp.dot(zk, w1_ref[...], precision=HIGHEST, preferred_element_type=F32)
                        + b1_ref[...]))
    hid = jnp.sin(fr * (jnp.dot(hid, w2_ref[...], precision=HIGHEST, preferred_element_type=F32)
                        + b2_ref[...]))
    hid = jnp.sin(fr * (jnp.dot(hid, w3_ref[...], precision=HIGHEST, preferred_element_type=F32)
                        + b3_ref[...]))
    filt = jnp.dot(hid, wo_ref[...], precision=HIGHEST, preferred_element_type=F32)
    t = zk[:, DECAY_POS_COL:DECAY_POS_COL + 1]
    mask = zk[:, MASK_POS_COL:MASK_POS_COL + 1]
    kap = filt * jnp.exp(-t * jnp.abs(dec_ref[0])) * mask
    kap_ref[...] = kap

    @pl.when(pl.program_id(0) == 0)
    def _():
        sum_ref[...] = jnp.zeros_like(sum_ref)

    sum_ref[...] += jnp.sum(jnp.abs(kap), axis=0, keepdims=True)


def _filter_spec_kernel(f_ref, k_ref, sum_ref, dbias_ref, c_ref, prev_ref, *, cb, nb):
    beta = pl.program_id(1)
    kap = k_ref[...]
    hi = kap.astype(BF16)
    lo = (kap - hi.astype(F32)).astype(BF16)
    a = (_dot(f_ref[...], hi) + _dot(f_ref[...], lo)) * (1.0 / sum_ref[...])
    ar = a[:cb] + jnp.where(beta == nb, dbias_ref[...], 0.0)
    ai = a[cb:]

    @pl.when(beta > 0)
    def _():
        kidx = lax.broadcasted_iota(jnp.int32, ar.shape, 0)
        sgn = (1 - 2 * (kidx % 2)).astype(F32)
        c_ref[0, :cb, :] = ar - sgn * prev_ref[cb:, :]
        c_ref[0, cb:, :] = ai + sgn * prev_ref[:cb, :]

    prev_ref[:cb, :] = ar
    prev_ref[cb:, :] = ai


def _hyena_filter_spectra(consts, w1, b1, w2, b2, w3, b3, freq, w_out, decay, d_bias, l, cb,
                          tr=1024, dtile=512):
    d = d_bias.shape[0]
    nb = l // cb
    tr = min(tr, l)
    hidden = w2.shape[0]
    w1p = jnp.zeros((POS_PAD, hidden), F32).at[:w1.shape[0]].set(w1)
    half_steps = l // tr
    vec = lambda a: a[None, :]
    const = lambda shape: _resident(shape, lambda i: (0,) * len(shape))
    kap, ksum = pl.pallas_call(
        _filter_taps_kernel,
        grid=(2 * l // tr,),
        in_specs=[
            pl.BlockSpec((tr, POS_PAD), lambda i: (i, 0)),
            const((POS_PAD, hidden)), const((1, hidden)),
            const((hidden, hidden)), const((1, hidden)),
            const((hidden, hidden)), const((1, hidden)),
            const((1, hidden)),
            pl.BlockSpec((hidden, d), lambda i: (0, jnp.where(i < half_steps, 1, 0))),
            pl.BlockSpec((1, 1, d), lambda i: (jnp.where(i < half_steps, 1, 0), 0, 0)),
        ],
        out_specs=[
            pl.BlockSpec((tr, d), lambda i: (i, 0)),
            pl.BlockSpec((1, d), lambda i: (0, 0)),
        ],
        out_shape=[
            jax.ShapeDtypeStruct((2 * l, d), F32),
            jax.ShapeDtypeStruct((1, d), F32),
        ],
        compiler_params=_cparams(("arbitrary",)),
        name="hyena_filter_taps",
    )(consts["zk"], w1p, vec(b1), w2, vec(b2), w3, vec(b3), vec(freq), w_out,
      decay[:, None, :])

    dtile = min(dtile, d)
    spec_kern = functools.partial(_filter_spec_kernel, cb=cb, nb=nb)
    return pl.pallas_call(
        spec_kern,
        grid=(d // dtile, 2 * nb),
        in_specs=[
            _resident((2 * cb, cb), lambda j, beta: (0, 0)),
            pl.BlockSpec((cb, dtile), lambda j, beta: (beta, j)),
            pl.BlockSpec((1, dtile), lambda j, beta: (0, j)),
            pl.BlockSpec((1, dtile), lambda j, beta: (0, j)),
        ],
        out_specs=pl.BlockSpec((1, 2 * cb, dtile), lambda j, beta: (jnp.maximum(beta - 1, 0), 0, j)),
        out_shape=jax.ShapeDtypeStruct((2 * nb - 1, 2 * cb, d), F32),
        scratch_shapes=[pltpu.VMEM((2 * cb, dtile), F32)],
        compiler_params=_cparams(("arbitrary", "arbitrary")),
        name="hyena_filter_spectra",
    )(consts["f_fwd"], kap, ksum, vec(d_bias))


def _hyena_consts(l, cb):
    pos_dim = 33
    bands = (pos_dim - 1) // 2
    t = np.linspace(0.0, 1.0, l)[:, None]
    f = np.linspace(1e-4, bands - 1, bands)
    w = 2.0 * math.pi * np.arange(l)[:, None] / l
    z = np.concatenate([t, np.cos(f * w), -np.sin(f * w)], axis=-1)
    r = np.arange(2 * l)
    valid = r > 0
    pos = np.where(valid, np.abs(r - l), 0)
    zk = np.zeros((2 * l, POS_PAD))
    zk[:, :pos_dim] = z[pos]
    zk[:, DECAY_POS_COL] = t[pos, 0]
    zk[:, MASK_POS_COL] = valid
    k = np.arange(cb)[:, None]
    n = np.arange(cb)[None, :]
    theta = (((2 * k + 1) * n) % (4 * cb)) * (math.pi / (2 * cb))
    cos, sin = np.cos(theta), np.sin(theta)
    f_fwd = np.concatenate([cos, -sin], axis=0)
    f_inv = np.concatenate([cos.T, -sin.T], axis=1) * (1.0 / cb)
    as_f32 = lambda a: jnp.asarray(a, dtype=F32)
    return {"zk": as_f32(zk), "f_fwd": as_f32(f_fwd).astype(BF16), "f_inv": as_f32(f_inv).astype(BF16)}


def _hyena_in_kernel(xp_ref, x_ref, xn_ref, g_ref, w_ref, b_ref, cw_ref, cb_ref, vg_ref, x0_ref,
                     *, dh, dc):
    in_seq_prev = jnp.where(pl.program_id(1) == 0, 0.0, 1.0)
    in_seq_next = jnp.where(pl.program_id(1) == pl.num_programs(1) - 1, 0.0, 1.0)
    tm = x_ref.shape[1]
    nrows = tm + 2 * SUBLANES
    rows = jnp.concatenate([xp_ref[0] * in_seq_prev, x_ref[0], xn_ref[0] * in_seq_next], axis=0)
    h = _rms(rows, g_ref[...]).astype(BF16)
    ridx = lax.broadcasted_iota(jnp.int32, (SUBLANES, dc), 0)
    first_row = jnp.where(ridx == 0, 1.0 - in_seq_prev, 0.0)
    last_row = jnp.where(ridx == SUBLANES - 1, 1.0 - in_seq_next, 0.0)

    def conv_part(col):
        cols = slice(col, col + dc)
        hw = _dot(h, w_ref[:, cols])
        b = b_ref[:, cols]
        cw = cw_ref[:, cols]
        y = (pltpu.roll(hw, 1, axis=0) * cw[0:1] + hw * cw[1:2]
             + pltpu.roll(hw, nrows - 1, axis=0) * cw[2:3])[SUBLANES:tm + SUBLANES]
        y = y + (b * (cw[0:1] + cw[1:2] + cw[2:3]) + cb_ref[:, cols])
        head = y[:SUBLANES] - first_row * (b * cw[0:1])
        tail = y[tm - SUBLANES:] - last_row * (b * cw[2:3])
        return jnp.concatenate([head, y[SUBLANES:tm - SUBLANES], tail], axis=0)

    for c in range(dh // dc):
        x0 = conv_part(c * dc)
        x1 = conv_part(dh + c * dc)
        v = conv_part(2 * dh + c * dc)
        x0_ref[0, :, c * dc:(c + 1) * dc] = x0
        vg_ref[0, :, c * dc:(c + 1) * dc] = (v * x1).astype(BF16)


def _hyena_conv_kernel(v_ref, x0_ref, c_ref, ff_ref, fi_ref, z_ref, vh_ref, yh_ref, *, cb, nb, rc):
    for j in range(nb):
        vh_ref[j] = _dot(ff_ref[...], v_ref[0, j * cb:(j + 1) * cb, :])
    for i in range(nb):
        for r0 in range(0, cb, rc):
            yr = None
            for j in range(nb):
                dlt = i - j + nb - 1
                cr = c_ref[dlt, r0:r0 + rc, :]
                ci = c_ref[dlt, cb + r0:cb + r0 + rc, :]
                vr = vh_ref[j, r0:r0 + rc, :]
                vi = vh_ref[j, cb + r0:cb + r0 + rc, :]
                pr = cr * vr - ci * vi
                pi = cr * vi + ci * vr
                yr, yi = (pr, pi) if yr is None else (yr + pr, yi + pi)
            yh_ref[i, r0:r0 + rc, :] = yr.astype(BF16)
            yh_ref[i, cb + r0:cb + r0 + rc, :] = yi.astype(BF16)
        y = _dot(fi_ref[...], yh_ref[i])
        z_ref[0, i * cb:(i + 1) * cb, :] = (y * x0_ref[0, i * cb:(i + 1) * cb, :]).astype(BF16)


def _hyena_layer(x, g, w_in, b_in, conv_w, conv_b, spectra, consts, cb, tm=1024, dc=256, dt=256,
                 rc=16):
    b, l, d = x.shape
    dh = conv_w.shape[1] // 3
    tm = min(tm, l)
    nb = l // cb
    hb = tm // SUBLANES
    nrow8 = l // SUBLANES
    in_kern = functools.partial(_hyena_in_kernel, dh=dh, dc=dc)
    const = lambda shape: _resident(shape, lambda bi, i: (0,) * len(shape))
    vg, x0 = pl.pallas_call(
        in_kern,
        grid=(b, l // tm),
        in_specs=[
            pl.BlockSpec((1, SUBLANES, d), lambda bi, i: (bi, jnp.maximum(i * hb - 1, 0), 0)),
            pl.BlockSpec((1, tm, d), lambda bi, i: (bi, i, 0)),
            pl.BlockSpec((1, SUBLANES, d), lambda bi, i: (bi, jnp.minimum((i + 1) * hb, nrow8 - 1), 0)),
            const((1, d)),
            const((d, 3 * dh)),
            const((1, 3 * dh)),
            const((SHORT_CONV, 3 * dh)),
            const((1, 3 * dh)),
        ],
        out_specs=[
            pl.BlockSpec((1, tm, dh), lambda bi, i: (bi, i, 0)),
            pl.BlockSpec((1, tm, dh), lambda bi, i: (bi, i, 0)),
        ],
        out_shape=[
            jax.ShapeDtypeStruct((b, l, dh), BF16),
            jax.ShapeDtypeStruct((b, l, dh), F32),
        ],
        compiler_params=_cparams(("arbitrary", "arbitrary")),
        name="hyena_in",
    )(x, x, x, g, w_in.astype(BF16), b_in[None, :], conv_w, conv_b[None, :])

    conv_kern = functools.partial(_hyena_conv_kernel, cb=cb, nb=nb, rc=rc)
    z = pl.pallas_call(
        conv_kern,
        grid=(dh // dt, b),
        in_specs=[
            pl.BlockSpec((1, l, dt), lambda j, bi: (bi, 0, j)),
            pl.BlockSpec((1, l, dt), lambda j, bi: (bi, 0, j)),
            _resident((2 * nb - 1, 2 * cb, dt), lambda j, bi: (0, 0, j)),
            _resident((2 * cb, cb), lambda j, bi: (0, 0)),
            _resident((cb, 2 * cb), lambda j, bi: (0, 0)),
        ],
        out_specs=pl.BlockSpec((1, l, dt), lambda j, bi: (bi, 0, j)),
        out_shape=jax.ShapeDtypeStruct((b, l, dh), BF16),
        scratch_shapes=[pltpu.VMEM((nb, 2 * cb, dt), F32), pltpu.VMEM((nb, 2 * cb, dt), BF16)],
        compiler_params=_cparams(("arbitrary", "arbitrary")),
        name="hyena_conv",
    )(vg, x0, spectra, consts["f_fwd"], consts["f_inv"])
    return z


def kernel(x, ffn_norm_g, mix_norm_g, ffn_w_in, ffn_w_out, attn_w_in, attn_q_gain, attn_k_gain,
           attn_w_out, hy_w_in, hy_b_in, hy_conv_w, hy_conv_b, hy_f_w1, hy_f_b1, hy_f_w2, hy_f_b2,
           hy_f_w3, hy_f_b3, hy_f_freq, hy_f_w_out, hy_decay, hy_d_bias, hy_w_out, hy_b_out,
           final_norm_g):
    b, l, d = x.shape
    depth = mix_norm_g.shape[0]
    head_dim = attn_q_gain.shape[1]
    cb = min(CONV_BLOCK, l)
    attn_consts = _attention_consts(l, head_dim, N_Q_HEADS, N_KV_HEADS)
    hy_consts = _hyena_consts(l, cb)
    ffn_w_in = ffn_w_in.astype(BF16)
    ffn_w_out = ffn_w_out.astype(BF16)
    gf = final_norm_g[None, :]
    tm = min(1024, b * l)

    def ffn(xx, i, k, final=False, pre=None):
        y = _ffn(xx.reshape(b * l, d), ffn_norm_g[i, k][None, :], ffn_w_in, ffn_w_out, i, k, gf,
                 final, pre=pre, tm=tm)
        return y.reshape(b, l, d)

    for i in range(depth):
        x = ffn(x, i, 0)
        j = i // 2
        g = mix_norm_g[i][None, :]
        pre = None
        if i % 2 == 0:
            x = _attention_layer(x, g, attn_w_in[j], attn_q_gain[j], attn_k_gain[j], attn_w_out[j],
                                 attn_consts)
        else:
            spectra = _hyena_filter_spectra(hy_consts, hy_f_w1[j], hy_f_b1[j], hy_f_w2[j],
                                            hy_f_b2[j], hy_f_w3[j], hy_f_b3[j], hy_f_freq[j],
                                            hy_f_w_out[j], hy_decay[j], hy_d_bias[j], l, cb)
            z = _hyena_layer(x, g, hy_w_in[j], hy_b_in[j], hy_conv_w[j], hy_conv_b[j], spectra,
                             hy_consts, cb)
            pre = (z.reshape(b * l, -1), hy_w_out[j].astype(BF16), hy_b_out[j][None, :])
        x = ffn(x, i, 1, final=(i == depth - 1), pre=pre)
    return x
```
